```python
import math
import jax, jax.numpy as jnp
from jax import lax
import numpy as np

D_MODEL = 1024
BATCH = 2
SEQ = 8192
DEPTH = 4
DEC_BATCH = 32
DEC_SEQ = 1
PAST_LEN = 8192
PAGE_SIZE = 128

ATT_HEADS = 8
ATT_HEAD_DIM = 64
ATT_WIDTH = ATT_HEADS * ATT_HEAD_DIM
MOBA_BLOCK = 256
MOBA_TOPK = 3
Q_BLOCK = 128
RPE_BUCKETS = 32
RPE_MAX_DIST = 128
HG_HEADS = 4
HG_KEY_DIM = 128
HG_VAL_DIM = 128
HG_KW = HG_HEADS * HG_KEY_DIM
HG_WIDTH = HG_HEADS * HG_VAL_DIM
HG_CHUNK = 64
MIX_WIDTH = ATT_WIDTH + HG_WIDTH
D_FF = 4 * D_MODEL
EPS = 1e-6
N_IN = 3 * ATT_WIDTH + 2 * HG_KW + 2 * HG_WIDTH
IN_SPLITS = (ATT_WIDTH, 2 * ATT_WIDTH, 3 * ATT_WIDTH, 3 * ATT_WIDTH + HG_KW,
             3 * ATT_WIDTH + 2 * HG_KW, 3 * ATT_WIDTH + 2 * HG_KW + HG_WIDTH)

kernel_name = 'hymba_moba_hgrn2_decode_step'

F32 = jnp.float32


def rms_norm(x, g):
    xf = x.astype(F32)
    y = xf * lax.rsqrt(jnp.mean(xf * xf, axis=-1, keepdims=True) + EPS) * g.astype(F32)
    return y.astype(x.dtype)


def rel_bucket(dist):
    max_exact = RPE_BUCKETS // 2
    d = jnp.maximum(dist, 0)
    large = max_exact + (jnp.log(jnp.maximum(d, max_exact).astype(F32) / max_exact)
                         / math.log(RPE_MAX_DIST / max_exact)
                         * (RPE_BUCKETS - max_exact)).astype(jnp.int32)
    large = jnp.minimum(large, RPE_BUCKETS - 1)
    return jnp.where(d < max_exact, d, large)


def moba_attention(q, k, v, q_pos0, rpe_table):
    B, Lq, H, dh = q.shape
    Lk = k.shape[1]
    pad_k = (-Lk) % MOBA_BLOCK
    k = jnp.pad(k, ((0, 0), (0, pad_k), (0, 0), (0, 0)))
    v = jnp.pad(v, ((0, 0), (0, pad_k), (0, 0), (0, 0)))
    nb = (Lk + pad_k) // MOBA_BLOCK
    k5 = k.reshape(B, nb, MOBA_BLOCK, H, dh)
    v5 = v.reshape(B, nb, MOBA_BLOCK, H, dh)
    kmean = jnp.mean(k5.astype(F32), axis=2)
    n_sel = min(MOBA_TOPK, nb)
    qb = min(Q_BLOCK, Lq)
    pad_q = (-Lq) % qb
    nq = (Lq + pad_q) // qb
    qp = jnp.pad(q, ((0, 0), (0, pad_q), (0, 0), (0, 0)))
    q_blocks = qp.reshape(B, nq, qb, H, dh).transpose(1, 0, 2, 3, 4)
    pos_blocks = (q_pos0 + jnp.arange(Lq + pad_q, dtype=jnp.int32)).reshape(nq, qb)
    bi = jnp.arange(B)[:, None, None]
    hi = jnp.arange(H)[None, :, None]
    rpe_t = rpe_table.astype(F32).T
    offs = jnp.arange(MOBA_BLOCK, dtype=jnp.int32)
    scale = dh ** -0.5

    def one_block(args):
        q_blk, p = args
        qh = q_blk.astype(F32).transpose(0, 2, 1, 3)
        own = jnp.minimum(p // MOBA_BLOCK, nb - 1)
        gate = jnp.einsum('bhqd,bnhd->bhqn', qh, kmean)
        fully_past = jnp.arange(nb)[None, :] < own[:, None]
        gate = jnp.where(fully_past, gate, -jnp.inf)
        _, sel = lax.top_k(gate, n_sel)
        idx_list = [sel[..., j] for j in range(n_sel)] + [jnp.broadcast_to(own, (B, H, qb))]
        valid_list = [j < own for j in range(n_sel)] + [own >= 0]
        logits = []
        for idx, valid in zip(idx_list, valid_list):
            kg = k5[bi, idx, :, hi, :]
            s = jnp.einsum('bhqd,bhqkd->bhqk', qh, kg.astype(F32)) * scale
            dist = p[None, None, :, None] - (idx[..., None] * MOBA_BLOCK + offs)
            bias = rpe_t[hi[..., None], rel_bucket(dist)]
            mask = (dist >= 0) & valid[None, None, :, None]
            logits.append(jnp.where(mask, s + bias, -jnp.inf))
        probs = jax.nn.softmax(jnp.concatenate(logits, axis=-1), axis=-1)
        outs = []
        for j, idx in enumerate(idx_list):
            vg = v5[bi, idx, :, hi, :]
            pj = probs[..., j * MOBA_BLOCK:(j + 1) * MOBA_BLOCK]
            outs.append(jnp.einsum('bhqk,bhqkd->bhqd', pj, vg.astype(F32)))
        out = outs[0]
        for o in outs[1:]:
            out = out + o
        return out.transpose(0, 2, 1, 3)

    o = lax.map(one_block, (q_blocks, pos_blocks))
    return o.transpose(1, 0, 2, 3, 4).reshape(B, nq * qb, H, dh)[:, :Lq]


def hgrn2_recurrence(q, logf, k, v, s0):
    B, L, H, _ = q.shape
    C = min(HG_CHUNK, L)
    pad = (-L) % C

    def prep(a):
        a = jnp.pad(a.astype(F32), ((0, 0), (0, pad), (0, 0), (0, 0)))
        return a.reshape(B, -1, C, H, a.shape[-1]).transpose(1, 0, 3, 2, 4)

    causal = jnp.tril(jnp.ones((C, C), dtype=bool))

    def step(S, inp):
        qc, gc, kc, vc = inp
        b = jnp.cumsum(gc, axis=2)
        o_inter = jnp.einsum('bhtd,bhde->bhte', qc * jnp.exp(b), S)
        diff = b[:, :, :, None, :] - b[:, :, None, :, :]
        decay = jnp.exp(jnp.where(causal[:, :, None], diff, -jnp.inf))
        a = jnp.einsum('bhtd,bhsd,bhtsd->bhts', qc, kc, decay)
        o = o_inter + jnp.einsum('bhts,bhse->bhte', a, vc)
        b_last = b[:, :, -1:, :]
        S = (jnp.exp(b_last[:, :, 0, :])[..., None] * S
             + jnp.einsum('bhsd,bhse->bhde', kc * jnp.exp(b_last - b), vc))
        return S, o

    S, o = lax.scan(step, s0.astype(F32), (prep(q), prep(logf), prep(k), prep(v)))
    o = o.transpose(1, 0, 3, 2, 4).reshape(B, L + pad, H, -1)[:, :L]
    return o, S


def layer_lower_bounds(lb_param):
    cs = jnp.cumsum(jax.nn.softmax(lb_param.astype(F32), axis=0), axis=0)
    return cs - cs[0:1]


def trunk(x, c, cache_k, cache_v, page_table, hg_state, w_ada, b_ada, g_pre_mix, g_post_mix,
          g_pre_ffn, g_post_ffn, w_in, lb_param, g_onorm, w_out, w_up, w_down, rpe_table):
    B, L, _ = x.shape
    past_len = 0 if page_table is None else page_table.shape[1] * PAGE_SIZE
    lb = layer_lower_bounds(lb_param)
    c_act = jax.nn.silu(c.astype(F32))
    k_rows, v_rows, states = [], [], []
    for l in range(DEPTH):
        mod = (c_act @ w_ada[l].astype(F32) + b_ada[l].astype(F32))[:, None, :]
        sh1, sc1, ga1, sh2, sc2, ga2 = jnp.split(mod, 6, axis=-1)
        h = (rms_norm(x, g_pre_mix[l]).astype(F32) * (1.0 + sc1) + sh1).astype(x.dtype)
        z = h @ w_in[l]
        qa, ka, va, qh, fh, ih, gh = jnp.split(z, IN_SPLITS, axis=-1)
        qa = qa.reshape(B, L, ATT_HEADS, ATT_HEAD_DIM)
        ka = ka.reshape(B, L, ATT_HEADS, ATT_HEAD_DIM)
        va = va.reshape(B, L, ATT_HEADS, ATT_HEAD_DIM)
        if page_table is None:
            k_all, v_all = ka, va
        else:
            pk = cache_k[l, page_table].reshape(B, past_len, ATT_HEADS, ATT_HEAD_DIM)
            pv = cache_v[l, page_table].reshape(B, past_len, ATT_HEADS, ATT_HEAD_DIM)
            k_all = jnp.concatenate([pk.astype(ka.dtype), ka], axis=1)
            v_all = jnp.concatenate([pv.astype(va.dtype), va], axis=1)
        o_att = moba_attention(qa, k_all, v_all, past_len, rpe_table).reshape(B, L, ATT_WIDTH)
        lb_l = lb[l]
        fpre = fh.astype(F32)
        logf = jnp.logaddexp(jnp.log(lb_l), jnp.log1p(-lb_l) + jax.nn.log_sigmoid(fpre))
        kk = (1.0 - lb_l) * jax.nn.sigmoid(-fpre)
        q_hg = jax.nn.silu(qh.astype(F32))
        shp = (B, L, HG_HEADS, HG_KEY_DIM)
        s0 = jnp.zeros((B, HG_HEADS, HG_KEY_DIM, HG_VAL_DIM), F32) if hg_state is None else hg_state[l]
        o_hg, s_new = hgrn2_recurrence(q_hg.reshape(shp), logf.reshape(shp), kk.reshape(shp),
                                       ih.reshape(B, L, HG_HEADS, HG_VAL_DIM), s0)
        o_hg = o_hg * lax.rsqrt(jnp.mean(o_hg * o_hg, axis=-1, keepdims=True) + EPS)
        o_hg = o_hg * g_onorm[l].astype(F32).reshape(HG_HEADS, HG_VAL_DIM)
        o_hg = o_hg.reshape(B, L, HG_WIDTH) * jax.nn.silu(gh.astype(F32))
        y = jnp.concatenate([o_att, o_hg], axis=-1).astype(x.dtype) @ w_out[l]
        x = (x.astype(F32) + ga1 * rms_norm(y, g_post_mix[l]).astype(F32)).astype(x.dtype)
        h2 = (rms_norm(x, g_pre_ffn[l]).astype(F32) * (1.0 + sc2) + sh2).astype(x.dtype)
        m = jnp.square(jax.nn.relu(h2 @ w_up[l])) @ w_down[l]
        x = (x.astype(F32) + ga2 * rms_norm(m, g_post_ffn[l]).astype(F32)).astype(x.dtype)
        k_rows.append(ka)
        v_rows.append(va)
        states.append(s_new.astype(x.dtype))
    return x, jnp.stack(k_rows), jnp.stack(v_rows), jnp.stack(states)


def setup_inputs(seed: int = 0) -> dict:
    key = jax.random.key(seed)
    ks = jax.random.split(key, 24)
    nrm = jax.random.normal
    n_pages = PAST_LEN // PAGE_SIZE
    n_used = DEC_BATCH * n_pages
    n_phys = n_used + max(1, n_used // 4)
    x_prompt = nrm(ks[0], (BATCH, SEQ, D_MODEL), F32)
    x_sample = nrm(ks[1], (DEC_BATCH, DEC_SEQ, D_MODEL), F32)
    c_prompt = nrm(ks[2], (BATCH, D_MODEL), F32)
    c_sample = nrm(ks[3], (DEC_BATCH, D_MODEL), F32)
    cache_k = nrm(ks[4], (DEPTH, n_phys, PAGE_SIZE, ATT_HEADS, ATT_HEAD_DIM), F32)
    cache_v = nrm(ks[5], (DEPTH, n_phys, PAGE_SIZE, ATT_HEADS, ATT_HEAD_DIM), F32)
    state_hgrn = 0.5 * nrm(ks[6], (DEPTH, DEC_BATCH, HG_HEADS, HG_KEY_DIM, HG_VAL_DIM), F32)
    page_table = jax.random.permutation(ks[7], n_phys)[:n_used].reshape(DEC_BATCH, n_pages).astype(jnp.int32)
    w_ada = nrm(ks[8], (DEPTH, D_MODEL, 6 * D_MODEL), F32) * (0.5 * D_MODEL ** -0.5)
    b_ada = 0.02 * nrm(ks[9], (DEPTH, 6 * D_MODEL), F32)
    g_pre_mix = 1.0 + 0.1 * nrm(ks[10], (DEPTH, D_MODEL), F32)
    g_post_mix = 1.0 + 0.1 * nrm(ks[11], (DEPTH, D_MODEL), F32)
    g_pre_ffn = 1.0 + 0.1 * nrm(ks[12], (DEPTH, D_MODEL), F32)
    g_post_ffn = 1.0 + 0.1 * nrm(ks[13], (DEPTH, D_MODEL), F32)
    w_in = nrm(ks[14], (DEPTH, D_MODEL, N_IN), F32) * D_MODEL ** -0.5
    lb_param = nrm(ks[15], (DEPTH, HG_KW), F32)
    g_onorm = 1.0 + 0.1 * nrm(ks[16], (DEPTH, HG_WIDTH), F32)
    w_out = nrm(ks[17], (DEPTH, MIX_WIDTH, D_MODEL), F32) * MIX_WIDTH ** -0.5
    w_up = nrm(ks[18], (DEPTH, D_MODEL, D_FF), F32) * D_MODEL ** -0.5
    w_down = nrm(ks[19], (DEPTH, D_FF, D_MODEL), F32) * D_FF ** -0.5
    rpe_table = 0.5 * nrm(ks[20], (RPE_BUCKETS, ATT_HEADS), F32)
    return {'x_prompt': x_prompt, 'x_sample': x_sample, 'c_prompt': c_prompt, 'c_sample': c_sample,
            'cache_k': cache_k, 'cache_v': cache_v, 'state_hgrn': state_hgrn, 'page_table': page_table,
            'w_ada': w_ada, 'b_ada': b_ada, 'g_pre_mix': g_pre_mix, 'g_post_mix': g_post_mix,
            'g_pre_ffn': g_pre_ffn, 'g_post_ffn': g_post_ffn, 'w_in': w_in, 'lb_param': lb_param,
            'g_onorm': g_onorm, 'w_out': w_out, 'w_up': w_up, 'w_down': w_down, 'rpe_table': rpe_table}


def reference(x_prompt, x_sample, c_prompt, c_sample, cache_k, cache_v, state_hgrn, page_table,
              w_ada, b_ada, g_pre_mix, g_post_mix, g_pre_ffn, g_post_ffn, w_in, lb_param, g_onorm,
              w_out, w_up, w_down, rpe_table):
    y_prompt, k_prompt, v_prompt, s_prompt = trunk(
        x_prompt, c_prompt, None, None, None, None, w_ada, b_ada, g_pre_mix, g_post_mix,
        g_pre_ffn, g_post_ffn, w_in, lb_param, g_onorm, w_out, w_up, w_down, rpe_table)
    y_sample, k_sample, v_sample, s_sample = trunk(
        x_sample, c_sample, cache_k, cache_v, page_table, state_hgrn, w_ada, b_ada, g_pre_mix,
        g_post_mix, g_pre_ffn, g_post_ffn, w_in, lb_param, g_onorm, w_out, w_up, w_down, rpe_table)
    return (y_prompt, y_sample, k_prompt, v_prompt, s_prompt, k_sample, v_sample, s_sample)
```

```python
import functools
import math

import numpy as np
import jax
import jax.numpy as jnp
from jax import lax
from jax.experimental import pallas as pl
from jax.experimental.pallas import tpu as pltpu

F32 = jnp.float32
BF16 = jnp.bfloat16

D_MODEL = 1024
DEPTH = 4
ATT_HEADS = 8
ATT_HEAD_DIM = 64
ATT_WIDTH = ATT_HEADS * ATT_HEAD_DIM
MOBA_BLOCK = 256
MOBA_TOPK = 3
PAGE_SIZE = 128
RPE_BUCKETS = 32
RPE_MAX_DIST = 128
HG_HEADS = 4
HG_DIM = 128
HG_WIDTH = HG_HEADS * HG_DIM
HG_SUB = 16
D_FF = 4 * D_MODEL
EPS = 1e-6

LANES = 128
HEAD_PAD = 128
GATE_SLOTS = 32
LOG2E = 1.4426950408889634
Q_SCALE = ATT_HEAD_DIM ** -0.5 * LOG2E
NEG = -30000.0
VMEM_LIMIT = 56 * 1024 * 1024

NT = (((1,), (1,)), ((), ()))


def _bucket_starts():
    max_exact = RPE_BUCKETS // 2
    d = np.arange(0, RPE_MAX_DIST + 1)
    dd = np.maximum(d, max_exact).astype(np.float32)
    large = max_exact + (np.log(dd / np.float32(max_exact)) / np.float32(math.log(RPE_MAX_DIST / max_exact))
                         * np.float32(RPE_BUCKETS - max_exact)).astype(np.int32)
    large = np.minimum(large, RPE_BUCKETS - 1)
    b = np.where(d < max_exact, d, large)
    return tuple(int(np.argmax(b >= k)) for k in range(RPE_BUCKETS))


BUCKET_STARTS = _bucket_starts()


def _params(sem):
    return pltpu.CompilerParams(dimension_semantics=sem, vmem_limit_bytes=VMEM_LIMIT)


def _rms(x, g):
    return x * lax.rsqrt(jnp.mean(x * x, axis=-1, keepdims=True) + EPS) * g


def _dot(a, b):
    return jnp.dot(a, b, preferred_element_type=F32)


def _split3(x):
    hi = x.astype(BF16)
    r = x - hi.astype(F32)
    mid = r.astype(BF16)
    lo = (r - mid.astype(F32)).astype(BF16)
    return hi, mid, lo


def _dot_exact_lhs(a, x):
    hi, mid, lo = _split3(x)
    return _dot(a, hi) + _dot(a, mid) + _dot(a, lo)


def _dot_exact_rhs(x, a):
    hi, mid, lo = _split3(x)
    return _dot(hi, a) + _dot(mid, a) + _dot(lo, a)


def _ada_kernel(c_ref, w_ref, b_ref, o_ref):
    c = c_ref[...]
    act = (c / (1.0 + jnp.exp(-c))).astype(BF16)
    o_ref[0, 0] = _dot(act, w_ref[0].astype(BF16)) + b_ref[0, 0]


def _ada_call(c_all, w_ada, b_ada):
    rows = c_all.shape[0]
    return pl.pallas_call(
        _ada_kernel,
        grid=(DEPTH, 6),
        in_specs=[pl.BlockSpec((rows, D_MODEL), lambda l, j: (0, 0)),
                  pl.BlockSpec((1, D_MODEL, D_MODEL), lambda l, j: (l, 0, j)),
                  pl.BlockSpec((1, 1, 1, D_MODEL), lambda l, j: (l, j, 0, 0))],
        out_specs=pl.BlockSpec((1, 1, rows, D_MODEL), lambda l, j: (l, j, 0, 0)),
        out_shape=jax.ShapeDtypeStruct((DEPTH, 6, rows, D_MODEL), F32),
        compiler_params=_params(("arbitrary", "arbitrary")),
        name="ada_mod",
    )(c_all, w_ada, b_ada.reshape(DEPTH, 6, 1, D_MODEL))


def _lb_kernel(p_ref, o_ref):
    p = p_ref[...]
    e = jnp.exp(p - jnp.max(p, axis=0, keepdims=True))
    sm = e / jnp.sum(e, axis=0, keepdims=True)
    acc = jnp.zeros((1, HG_WIDTH), F32)
    for l in range(DEPTH):
        o_ref[l:l + 1, :] = acc
        if l + 1 < DEPTH:
            acc = acc + sm[l + 1:l + 2, :]


def _lb_call(lb_param):
    return pl.pallas_call(
        _lb_kernel,
        out_shape=jax.ShapeDtypeStruct((DEPTH, HG_WIDTH), F32),
        name="hgrn_lower_bounds",
    )(lb_param)


def _bias_lookup(d, table):
    val = table(RPE_BUCKETS - 1)
    for b in range(RPE_BUCKETS - 2, -1, -1):
        val = jnp.where(d < BUCKET_STARTS[b + 1], table(b), val)
    return val * LOG2E


def _bias_tile_kernel(rpe_ref, o_ref):
    h = pl.program_id(0)
    kk = lax.broadcasted_iota(jnp.int32, (MOBA_BLOCK, MOBA_BLOCK), 0)
    qq = lax.broadcasted_iota(jnp.int32, (MOBA_BLOCK, MOBA_BLOCK), 1)
    d = qq - kk
    table = lambda b: rpe_ref[h * RPE_BUCKETS + b]
    o_ref[0, 0] = jnp.where(d >= 0, _bias_lookup(d, table), NEG)
    o_ref[0, 1] = _bias_lookup(d + MOBA_BLOCK, table)


def _bias_tile_call(rpe_flat):
    return pl.pallas_call(
        _bias_tile_kernel,
        grid=(ATT_HEADS,),
        in_specs=[pl.BlockSpec(memory_space=pltpu.SMEM)],
        out_specs=pl.BlockSpec((1, 2, MOBA_BLOCK, MOBA_BLOCK), lambda h: (h, 0, 0, 0)),
        out_shape=jax.ShapeDtypeStruct((ATT_HEADS, 2, MOBA_BLOCK, MOBA_BLOCK), F32),
        compiler_params=_params(("arbitrary",)),
        name="rpe_bias_tiles",
    )(rpe_flat)


def _dec_bias_kernel(rpe_ref, o_ref, *, past_len):
    pos = lax.broadcasted_iota(jnp.int32, (past_len, LANES), 0)
    d = past_len - pos
    table = lambda b: rpe_ref[b:b + 1, :]
    o_ref[...] = _bias_lookup(d, table)


def _dec_bias_call(rpe_pad, past_len):
    return pl.pallas_call(
        functools.partial(_dec_bias_kernel, past_len=past_len),
        out_shape=jax.ShapeDtypeStruct((past_len, LANES), F32),
        compiler_params=pltpu.CompilerParams(vmem_limit_bytes=VMEM_LIMIT),
        name="rpe_bias_decode",
    )(rpe_pad)


def _in_proj_kernel(x_ref, g_ref, sc_ref, sh_ref, wqT_ref, wk_ref, wkp_ref, wv_ref, wvT_ref, whg_ref,
                    qT_ref, k_ref, kaug_ref, kmean_ref, v_ref, vT_ref, zhg_ref, *, blocks_per_seq):
    x = x_ref[...]
    tm = x.shape[0]
    h = (_rms(x, g_ref[...]) * (1.0 + sc_ref[0]) + sh_ref[0]).astype(BF16)
    qT = lax.dot_general(wqT_ref[...], h, NT, preferred_element_type=F32)
    qT_ref[...] = (qT * Q_SCALE).astype(BF16)
    k_ref[...] = _dot(h, wk_ref[...])
    v_ref[...] = _dot(h, wv_ref[...])
    vT_ref[...] = lax.dot_general(wvT_ref[...], h, NT, preferred_element_type=F32).astype(BF16)
    zhg_ref[...] = _dot(h, whg_ref[...])
    kp = _dot(h, wkp_ref[...])
    nb = tm // MOBA_BLOCK
    for r in range(nb):
        kmean_ref[r] = jnp.mean(kp[r * MOBA_BLOCK:(r + 1) * MOBA_BLOCK], axis=0, keepdims=True)
    row = lax.broadcasted_iota(jnp.int32, kp.shape, 0)
    lane = lax.broadcasted_iota(jnp.int32, kp.shape, 1) % HEAD_PAD
    blk = (pl.program_id(0) * nb + row // MOBA_BLOCK) % blocks_per_seq
    onehot = (lane == blk + ATT_HEAD_DIM) | (lane == blk + ATT_HEAD_DIM + GATE_SLOTS)
    kaug_ref[...] = jnp.where(onehot, 1.0, kp).astype(BF16)


def _in_proj_call(x2, g, sc, sh, w, seq_len, tm):
    t = x2.shape[0]
    tiles_per_seq = seq_len // tm
    nbt = tm // MOBA_BLOCK
    hp = ATT_HEADS * HEAD_PAD
    row = lambda i: (i, 0)
    col = lambda i: (0, i)
    fixed = lambda i: (0, 0)
    mod = lambda i: (i // tiles_per_seq, 0, 0)
    wspec = lambda a: pl.BlockSpec(a.shape, fixed)
    return pl.pallas_call(
        functools.partial(_in_proj_kernel, blocks_per_seq=seq_len // MOBA_BLOCK),
        grid=(t // tm,),
        in_specs=[pl.BlockSpec((tm, D_MODEL), row),
                  pl.BlockSpec((1, D_MODEL), fixed),
                  pl.BlockSpec((1, 1, D_MODEL), mod),
                  pl.BlockSpec((1, 1, D_MODEL), mod),
                  wspec(w["wqT"]), wspec(w["wk"]), wspec(w["wkp"]), wspec(w["wv"]), wspec(w["wvT"]),
                  wspec(w["whg"])],
        out_specs=[pl.BlockSpec((ATT_WIDTH, tm), col),
                   pl.BlockSpec((tm, ATT_WIDTH), row),
                   pl.BlockSpec((tm, hp), row),
                   pl.BlockSpec((nbt, 1, hp), lambda i: (i, 0, 0)),
                   pl.BlockSpec((tm, ATT_WIDTH), row),
                   pl.BlockSpec((ATT_WIDTH, tm), col),
                   pl.BlockSpec((tm, 4 * HG_WIDTH), row)],
        out_shape=[jax.ShapeDtypeStruct((ATT_WIDTH, t), BF16),
                   jax.ShapeDtypeStruct((t, ATT_WIDTH), F32),
                   jax.ShapeDtypeStruct((t, hp), BF16),
                   jax.ShapeDtypeStruct((t // MOBA_BLOCK, 1, hp), F32),
                   jax.ShapeDtypeStruct((t, ATT_WIDTH), F32),
                   jax.ShapeDtypeStruct((ATT_WIDTH, t), BF16),
                   jax.ShapeDtypeStruct((t, 4 * HG_WIDTH), F32)],
        compiler_params=_params(("arbitrary",)),
        name="in_proj",
    )(x2, g, sc, sh, w["wqT"], w["wk"], w["wkp"], w["wv"], w["wvT"], w["whg"])


def _gate_kernel(rpe_ref, kmean_ref, qT_ref, o_ref, *, nbs):
    own = pl.program_id(1)
    tq = qT_ref.shape[1]
    n = lax.broadcasted_iota(jnp.int32, (nbs, tq), 0)
    far = n <= own - 2
    for h in range(ATT_HEADS):
        q = qT_ref[h * ATT_HEAD_DIM:(h + 1) * ATT_HEAD_DIM, :]
        km = kmean_ref[0][:, h * HEAD_PAD:h * HEAD_PAD + ATT_HEAD_DIM].astype(BF16)
        gm = jnp.where(n < own, _dot(km, q), -jnp.inf)
        sel = jnp.zeros((nbs, tq), jnp.bool_)
        for j in range(MOBA_TOPK):
            mx = jnp.max(gm, axis=0, keepdims=True)
            idx = jnp.min(jnp.where(gm == mx, n, nbs), axis=0, keepdims=True)
            hit = n == idx
            sel = sel | (hit & (j < own))
            gm = jnp.where(hit, -jnp.inf, gm)
        c = jnp.full((nbs, tq), rpe_ref[h * RPE_BUCKETS + RPE_BUCKETS - 1] * LOG2E, F32)
        c_hi = c.astype(BF16).astype(F32)
        c_lo = c - c_hi
        p_hi = jnp.where(sel, jnp.where(far, c_hi, 0.0), NEG)
        p_hi = jnp.where(n >= own, 0.0, p_hi)
        p_lo = jnp.where(sel & far, c_lo, 0.0)
        base = h * HEAD_PAD
        o_ref[base:base + ATT_HEAD_DIM, :] = q
        o_ref[base + ATT_HEAD_DIM:base + HEAD_PAD, :] = jnp.zeros((HEAD_PAD - ATT_HEAD_DIM, tq), BF16)
        o_ref[base + ATT_HEAD_DIM:base + ATT_HEAD_DIM + nbs, :] = p_hi.astype(BF16)
        o_ref[base + ATT_HEAD_DIM + GATE_SLOTS:base + ATT_HEAD_DIM + GATE_SLOTS + nbs, :] = p_lo.astype(BF16)


def _gate_call(rpe_flat, kmean, qT, batch, seq_len):
    nbs = seq_len // MOBA_BLOCK
    hp = ATT_HEADS * HEAD_PAD
    t = qT.shape[1]
    return pl.pallas_call(
        functools.partial(_gate_kernel, nbs=nbs),
        grid=(batch, nbs),
        in_specs=[pl.BlockSpec(memory_space=pltpu.SMEM),
                  pl.BlockSpec((1, nbs, hp), lambda b, i: (b, 0, 0)),
                  pl.BlockSpec((ATT_WIDTH, MOBA_BLOCK), lambda b, i: (0, b * nbs + i))],
        out_specs=pl.BlockSpec((hp, MOBA_BLOCK), lambda b, i: (0, b * nbs + i)),
        out_shape=jax.ShapeDtypeStruct((hp, t), BF16),
        compiler_params=_params(("arbitrary", "arbitrary")),
        name="moba_gate",
    )(rpe_flat, kmean, qT)


def _att_kernel(q_ref, k_ref, vT_ref, bias_ref, o_ref):
    qi = pl.program_id(2)
    q = q_ref[...]

    def block(n, carry, bias):
        m, l, acc = carry
        start = pl.multiple_of(n * MOBA_BLOCK, MOBA_BLOCK)
        s = _dot(k_ref[pl.ds(start, MOBA_BLOCK), :], q)
        if bias is not None:
            s = s + bias
        m_new = jnp.maximum(m, jnp.max(s, axis=0, keepdims=True))
        alpha = jnp.exp2(m - m_new)
        p = jnp.exp2(s - m_new)
        l = alpha * l + jnp.sum(p, axis=0, keepdims=True)
        acc = alpha * acc + _dot(vT_ref[:, pl.ds(start, MOBA_BLOCK)], p.astype(BF16))
        return m_new, l, acc

    tq = q.shape[1]
    init = (jnp.full((1, tq), -jnp.inf, F32), jnp.zeros((1, tq), F32),
            jnp.zeros((ATT_HEAD_DIM, tq), F32))
    carry = block(qi, init, bias_ref[0, 0])
    carry = lax.cond(qi >= 1, lambda c: block(qi - 1, c, bias_ref[0, 1]), lambda c: c, carry)
    carry = lax.fori_loop(0, jnp.maximum(qi - 1, 0), lambda n, c: block(n, c, None), carry)
    _, l, acc = carry
    o_ref[...] = (acc / l).astype(BF16)


def _att_call(qaug, kaug, vT, bias_tiles, batch, seq_len):
    nq = seq_len // MOBA_BLOCK
    t = qaug.shape[1]
    return pl.pallas_call(
        _att_kernel,
        grid=(batch, ATT_HEADS, nq),
        in_specs=[pl.BlockSpec((HEAD_PAD, MOBA_BLOCK), lambda b, h, i: (h, b * nq + i)),
                  pl.BlockSpec((seq_len, HEAD_PAD), lambda b, h, i: (b, h)),
                  pl.BlockSpec((ATT_HEAD_DIM, seq_len), lambda b, h, i: (h, b)),
                  pl.BlockSpec((1, 2, MOBA_BLOCK, MOBA_BLOCK), lambda b, h, i: (h, 0, 0, 0))],
        out_specs=pl.BlockSpec((ATT_HEAD_DIM, MOBA_BLOCK), lambda b, h, i: (h, b * nq + i)),
        out_shape=jax.ShapeDtypeStruct((ATT_WIDTH, t), BF16),
        compiler_params=_params(("arbitrary", "arbitrary", "arbitrary")),
        name="moba_attention",
    )(qaug, kaug, vT, bias_tiles)


def _log_forget(fpre, lb):
    log_sig = jnp.minimum(fpre, 0.0) - jnp.log1p(jnp.exp(-jnp.abs(fpre)))
    a1 = jnp.log(lb)
    a2 = jnp.log1p(-lb) + log_sig
    return jnp.maximum(a1, a2) + jnp.log1p(jnp.exp(-jnp.abs(a1 - a2)))


def _hg_kernel(qh_ref, fh_ref, ih_ref, gh_ref, lb_ref, gon_ref, o_ref, sT_ref):
    @pl.when(pl.program_id(2) == 0)
    def _():
        sT_ref[...] = jnp.zeros_like(sT_ref)

    ts = qh_ref.shape[0]
    n_sub = ts // HG_SUB
    lb = lb_ref[...]
    fpre = fh_ref[...]
    qh = qh_ref[...]
    v = ih_ref[...]
    logf = _log_forget(fpre, lb)
    kk = (1.0 - lb) / (1.0 + jnp.exp(fpre))
    q = qh / (1.0 + jnp.exp(-qh))

    r = lax.broadcasted_iota(jnp.int32, (ts, ts), 0)
    c = lax.broadcasted_iota(jnp.int32, (ts, ts), 1)
    same = (r // HG_SUB) == (c // HG_SUB)
    tri = jnp.where(same & (c <= r), 1.0, 0.0).astype(BF16)
    ones = jnp.where(same, 1.0, 0.0).astype(BF16)
    b = _dot_exact_lhs(tri, logf)
    b_end = _dot_exact_lhs(ones, logf)
    q_dec = (q * jnp.exp(b)).astype(BF16)
    k_dec = (kk * jnp.exp(b_end - b)).astype(BF16)
    decay = jnp.exp(b_end)

    tmod = lax.broadcasted_iota(jnp.int32, (ts, HG_DIM), 0) % HG_SUB
    o = jnp.zeros((ts, HG_DIM), F32)
    for dlt in range(HG_SUB):
        if dlt == 0:
            bs, ks, vs = b, kk, v
        else:
            bs = pltpu.roll(b, dlt, 0)
            ks = pltpu.roll(kk, dlt, 0)
            vs = pltpu.roll(v, dlt, 0)
        e = jnp.exp(jnp.where(tmod >= dlt, b - bs, -1e30))
        a = jnp.sum(q * ks * e, axis=-1, keepdims=True)
        o = o + a * vs

    vT = v.T
    lane_sub = lax.broadcasted_iota(jnp.int32, (HG_DIM, ts), 1) // HG_SUB
    sT = sT_ref[0, 0]
    inter = []
    for j in range(n_sub):
        rows = slice(j * HG_SUB, (j + 1) * HG_SUB)
        inter.append(lax.dot_general(q_dec[rows], sT.astype(BF16), NT, preferred_element_type=F32))
        vT_j = jnp.where(lane_sub == j, vT, 0.0).astype(BF16)
        sT = sT * decay[j * HG_SUB:j * HG_SUB + 1, :] + _dot(vT_j, k_dec)
    sT_ref[0, 0] = sT
    o = o + jnp.concatenate(inter, axis=0)

    gh = gh_ref[...]
    o = _rms(o, gon_ref[...]) * (gh / (1.0 + jnp.exp(-gh)))
    o_ref[...] = o.astype(BF16)


def _hg_call(zhg, lb_l, gon_l, batch, seq_len, ts):
    t = zhg.shape[0]
    nt = seq_len // ts
    part = lambda p: pl.BlockSpec((ts, HG_DIM), lambda b, h, i: (b * nt + i, p * HG_HEADS + h))
    vec = pl.BlockSpec((1, HG_DIM), lambda b, h, i: (0, h))
    return pl.pallas_call(
        _hg_kernel,
        grid=(batch, HG_HEADS, nt),
        in_specs=[part(0), part(1), part(2), part(3), vec, vec],
        out_specs=[pl.BlockSpec((ts, HG_DIM), lambda b, h, i: (b * nt + i, h)),
                   pl.BlockSpec((1, 1, HG_DIM, HG_DIM), lambda b, h, i: (b, h, 0, 0))],
        out_shape=[jax.ShapeDtypeStruct((t, HG_WIDTH), BF16),
                   jax.ShapeDtypeStruct((batch, HG_HEADS, HG_DIM, HG_DIM), F32)],
        compiler_params=_params(("arbitrary", "arbitrary", "arbitrary")),
        name="hgrn2_prompt",
    )(zhg, zhg, zhg, zhg, lb_l, gon_l)


def _post_kernel(x_ref, oa_ref, oh_ref, ga1_ref, sc2_ref, sh2_ref, ga2_ref, gpm_ref, gpf_ref, gqf_ref,
                 woa_ref, woh_ref, wup_ref, wdn_ref, out_ref, *, ff_chunk):
    x = x_ref[...]
    y = _dot(oa_ref[...], woa_ref[...]) + _dot(oh_ref[...], woh_ref[...])
    x1 = x + ga1_ref[0] * _rms(y, gpm_ref[...])
    h2 = (_rms(x1, gpf_ref[...]) * (1.0 + sc2_ref[0]) + sh2_ref[0]).astype(BF16)
    m = jnp.zeros(x.shape, F32)
    for c in range(D_FF // ff_chunk):
        cols = slice(c * ff_chunk, (c + 1) * ff_chunk)
        u = jnp.square(jnp.maximum(_dot(h2, wup_ref[:, cols]), 0.0)).astype(BF16)
        m = m + _dot(u, wdn_ref[cols, :])
    out_ref[...] = x1 + ga2_ref[0] * _rms(m, gqf_ref[...])


def _post_call(x2, oa, oh, mods, gains, w, rows_per_mod, tm):
    t = x2.shape[0]
    row = lambda i: (i, 0)
    fixed = lambda i: (0, 0)
    mod_rows = mods[0].shape[1]
    mod = pl.BlockSpec((1, mod_rows, D_MODEL), lambda i: ((i * tm) // rows_per_mod, 0, 0))
    gain = pl.BlockSpec((1, D_MODEL), fixed)
    wspec = lambda a: pl.BlockSpec(a.shape, fixed)
    return pl.pallas_call(
        functools.partial(_post_kernel, ff_chunk=1024),
        grid=(t // tm,),
        in_specs=[pl.BlockSpec((tm, D_MODEL), row),
                  pl.BlockSpec((tm, ATT_WIDTH), row),
                  pl.BlockSpec((tm, HG_WIDTH), row),
                  mod, mod, mod, mod, gain, gain, gain,
                  wspec(w["woa"]), wspec(w["woh"]), wspec(w["wup"]), wspec(w["wdn"])],
        out_specs=pl.BlockSpec((tm, D_MODEL), row),
        out_shape=jax.ShapeDtypeStruct((t, D_MODEL), F32),
        compiler_params=_params(("arbitrary",)),
        name="out_proj_mlp",
    )(x2, oa, oh, *mods, *gains, w["woa"], w["woh"], w["wup"], w["wdn"])


def _dec_in_kernel(x_ref, g_ref, sc_ref, sh_ref, w_ref, z_ref):
    h = (_rms(x_ref[...], g_ref[...]) * (1.0 + sc_ref[...]) + sh_ref[...]).astype(BF16)
    z_ref[...] = _dot(h, w_ref[...])


def _dec_in_call(x2, g, sc, sh, w_in_l):
    rows = x2.shape[0]
    n_in = w_in_l.shape[1]
    tn = 512
    full = pl.BlockSpec((rows, D_MODEL), lambda j: (0, 0))
    return pl.pallas_call(
        _dec_in_kernel,
        grid=(n_in // tn,),
        in_specs=[full, pl.BlockSpec((1, D_MODEL), lambda j: (0, 0)), full, full,
                  pl.BlockSpec((D_MODEL, tn), lambda j: (0, j))],
        out_specs=pl.BlockSpec((rows, tn), lambda j: (0, j)),
        out_shape=jax.ShapeDtypeStruct((rows, n_in), F32),
        compiler_params=_params(("arbitrary",)),
        name="in_proj_sample",
    )(x2, g, sc, sh, w_in_l)


def _dec_att_kernel(pt_ref, q_ref, kn_ref, vn_ref, bias_ref, rpe0_ref, *refs, pages_per_step, n_blocks):
    k_refs = refs[:pages_per_step]
    v_refs = refs[pages_per_step:2 * pages_per_step]
    o_ref = refs[2 * pages_per_step]
    m_s, l_s, g_s, acc_s, qbd_s = refs[2 * pages_per_step + 1:]
    j = pl.program_id(1)
    bps = pages_per_step // 2

    lane = lax.broadcasted_iota(jnp.int32, (LANES, ATT_WIDTH), 1)
    hrow = lax.broadcasted_iota(jnp.int32, (LANES, ATT_WIDTH), 0)
    expand = jnp.where(lane // ATT_HEAD_DIM == hrow, 1.0, 0.0).astype(BF16)
    qs = q_ref[0] * Q_SCALE

    def collapse():
        d = lax.broadcasted_iota(jnp.int32, (ATT_WIDTH, LANES), 0)
        hcol = lax.broadcasted_iota(jnp.int32, (ATT_WIDTH, LANES), 1)
        return jnp.where(d // ATT_HEAD_DIM == hcol, 1.0, 0.0)

    @pl.when(j == 0)
    def _():
        m_s[...] = jnp.zeros_like(m_s)
        l_s[...] = jnp.zeros_like(l_s)
        g_s[...] = jnp.zeros_like(g_s)
        acc_s[...] = jnp.zeros_like(acc_s)
        r = lax.broadcasted_iota(jnp.int32, (ATT_WIDTH, ATT_WIDTH), 0)
        c = lax.broadcasted_iota(jnp.int32, (ATT_WIDTH, ATT_WIDTH), 1)
        q_col = jnp.sum(jnp.where(r == c, jnp.broadcast_to(qs, (ATT_WIDTH, ATT_WIDTH)), 0.0),
                        axis=1, keepdims=True)
        qbd_s[...] = (collapse() * q_col).astype(BF16)

    q_bd = qbd_s[...]
    for blk in range(bps):
        s_raw = [_dot(k_refs[2 * blk + hf][0, 0].astype(BF16), q_bd) for hf in range(2)]
        g = jnp.sum(s_raw[0], axis=0, keepdims=True) + jnp.sum(s_raw[1], axis=0, keepdims=True)
        s = [s_raw[hf] + bias_ref[(2 * blk + hf) * PAGE_SIZE:(2 * blk + hf + 1) * PAGE_SIZE, :]
             for hf in range(2)]
        m = jnp.maximum(jnp.max(s[0], axis=0, keepdims=True), jnp.max(s[1], axis=0, keepdims=True))
        l = jnp.zeros((1, LANES), F32)
        acc = jnp.zeros((1, ATT_WIDTH), F32)
        for hf in range(2):
            p = jnp.exp2(s[hf] - m)
            l = l + jnp.sum(p, axis=0, keepdims=True)
            p_wide = _dot(p.astype(BF16), expand)
            acc = acc + jnp.sum(p_wide * v_refs[2 * blk + hf][0, 0], axis=0, keepdims=True)
        row = j * bps + blk
        m_s[pl.ds(row, 1), :] = m
        l_s[pl.ds(row, 1), :] = l
        g_s[pl.ds(row, 1), :] = g
        acc_s[pl.ds(row, 1), :] = acc

    @pl.when(j == pl.num_programs(1) - 1)
    def _():
        s_new = _dot_exact_rhs(qs * kn_ref[0], collapse().astype(BF16)) + rpe0_ref[...] * LOG2E
        m_s[n_blocks:n_blocks + 1, :] = s_new
        l_s[n_blocks:n_blocks + 1, :] = jnp.ones((1, LANES), F32)
        acc_s[n_blocks:n_blocks + 1, :] = vn_ref[0]
        rows = m_s.shape[0]
        n = lax.broadcasted_iota(jnp.int32, (rows, LANES), 0)
        gm = jnp.where(n < n_blocks, g_s[...], -jnp.inf)
        sel = n == n_blocks
        for _ in range(min(MOBA_TOPK, n_blocks)):
            mx = jnp.max(gm, axis=0, keepdims=True)
            idx = jnp.min(jnp.where(gm == mx, n, rows), axis=0, keepdims=True)
            hit = n == idx
            sel = sel | hit
            gm = jnp.where(hit, -jnp.inf, gm)
        mm = jnp.where(sel, m_s[...], -jnp.inf)
        w = jnp.exp2(mm - jnp.max(mm, axis=0, keepdims=True))
        den = jnp.sum(w * l_s[...], axis=0, keepdims=True)
        num = jnp.sum(_dot_exact_rhs(w, expand) * acc_s[...], axis=0, keepdims=True)
        o_ref[0] = (num / _dot_exact_rhs(den, expand)).astype(BF16)


def _dec_att_call(page_table, q3, kn3, vn3, dec_bias, rpe0, cache_k4, cache_v4, layer, pages_per_step):
    n_seq, n_pages = page_table.shape
    n_blocks = n_pages * PAGE_SIZE // MOBA_BLOCK
    steps = n_pages // pages_per_step
    rows = ((n_blocks + 1 + 7) // 8) * 8
    vec = pl.BlockSpec((1, 1, ATT_WIDTH), lambda b, j, pt: (b, 0, 0))

    def page_spec(p):
        return pl.BlockSpec((1, 1, PAGE_SIZE, ATT_WIDTH),
                            lambda b, j, pt: (layer, pt[b, j * pages_per_step + p], 0, 0))

    grid_spec = pltpu.PrefetchScalarGridSpec(
        num_scalar_prefetch=1,
        grid=(n_seq, steps),
        in_specs=[vec, vec, vec,
                  pl.BlockSpec((pages_per_step * PAGE_SIZE, LANES), lambda b, j, pt: (j, 0)),
                  pl.BlockSpec((1, LANES), lambda b, j, pt: (0, 0))]
                 + [page_spec(p) for p in range(pages_per_step)]
                 + [page_spec(p) for p in range(pages_per_step)],
        out_specs=pl.BlockSpec((1, 1, ATT_WIDTH), lambda b, j, pt: (b, 0, 0)),
        scratch_shapes=[pltpu.VMEM((rows, LANES), F32), pltpu.VMEM((rows, LANES), F32),
                        pltpu.VMEM((rows, LANES), F32), pltpu.VMEM((rows, ATT_WIDTH), F32),
                        pltpu.VMEM((ATT_WIDTH, LANES), BF16)],
    )
    return pl.pallas_call(
        functools.partial(_dec_att_kernel, pages_per_step=pages_per_step, n_blocks=n_blocks),
        grid_spec=grid_spec,
        out_shape=jax.ShapeDtypeStruct((n_seq, 1, ATT_WIDTH), BF16),
        compiler_params=_params(("arbitrary", "arbitrary")),
        name="moba_attention_sample",
    )(page_table, q3, kn3, vn3, dec_bias, rpe0,
      *([cache_k4] * pages_per_step), *([cache_v4] * pages_per_step))


def _dec_hg_kernel(qh_ref, fh_ref, ih_ref, gh_ref, lb_ref, gon_ref, s_ref, o_ref, so_ref):
    r = lax.broadcasted_iota(jnp.int32, (HG_DIM, HG_DIM), 0)
    c = lax.broadcasted_iota(jnp.int32, (HG_DIM, HG_DIM), 1)
    eye = r == c

    def column(x):
        return jnp.sum(jnp.where(eye, jnp.broadcast_to(x, (HG_DIM, HG_DIM)), 0.0), axis=1, keepdims=True)

    for h in range(HG_HEADS):
        cols = slice(h * HG_DIM, (h + 1) * HG_DIM)
        lb = lb_ref[:, cols]
        fpre = fh_ref[0][:, cols]
        qh = qh_ref[0][:, cols]
        gh = gh_ref[0][:, cols]
        v = ih_ref[0][:, cols]
        f = jnp.exp(_log_forget(fpre, lb))
        kk = (1.0 - lb) / (1.0 + jnp.exp(fpre))
        q = qh / (1.0 + jnp.exp(-qh))
        s_new = column(f) * s_ref[0, h] + column(kk) * v
        so_ref[0, h] = s_new
        o = jnp.sum(column(q) * s_new, axis=0, keepdims=True)
        o = _rms(o, gon_ref[:, cols]) * (gh / (1.0 + jnp.exp(-gh)))
        o_ref[0, :, cols] = o.astype(BF16)


def _dec_hg_call(qh3, fh3, ih3, gh3, lb_l, gon_l, state_l):
    n_seq = qh3.shape[0]
    vec = pl.BlockSpec((1, 1, HG_WIDTH), lambda b: (b, 0, 0))
    par = pl.BlockSpec((1, HG_WIDTH), lambda b: (0, 0))
    st = pl.BlockSpec((1, HG_HEADS, HG_DIM, HG_DIM), lambda b: (b, 0, 0, 0))
    return pl.pallas_call(
        _dec_hg_kernel,
        grid=(n_seq,),
        in_specs=[vec, vec, vec, vec, par, par, st],
        out_specs=[vec, st],
        out_shape=[jax.ShapeDtypeStruct((n_seq, 1, HG_WIDTH), BF16),
                   jax.ShapeDtypeStruct(state_l.shape, F32)],
        compiler_params=_params(("arbitrary",)),
        name="hgrn2_sample",
    )(qh3, fh3, ih3, gh3, lb_l, gon_l, state_l)


def _layer_weights(w_in, w_out, w_up, w_down, l):
    a = ATT_WIDTH
    wi = w_in[l]
    wq, wk, wv, whg = wi[:, :a], wi[:, a:2 * a], wi[:, 2 * a:3 * a], wi[:, 3 * a:]
    wkp = jnp.pad(wk.reshape(D_MODEL, ATT_HEADS, ATT_HEAD_DIM),
                  ((0, 0), (0, 0), (0, HEAD_PAD - ATT_HEAD_DIM))).reshape(D_MODEL, ATT_HEADS * HEAD_PAD)
    return {"wqT": wq.T, "wk": wk, "wkp": wkp, "wv": wv, "wvT": wv.T, "whg": whg, "w_in": wi,
            "woa": w_out[l, :a], "woh": w_out[l, a:], "wup": w_up[l], "wdn": w_down[l]}


def kernel(x_prompt, x_sample, c_prompt, c_sample, cache_k, cache_v, state_hgrn, page_table, w_ada, b_ada,
           g_pre_mix, g_post_mix, g_pre_ffn, g_post_ffn, w_in, lb_param, g_onorm, w_out, w_up, w_down,
           rpe_table):
    batch, seq_len, _ = x_prompt.shape
    n_seq, dec_seq, _ = x_sample.shape
    n_pages = page_table.shape[1]
    past_len = n_pages * PAGE_SIZE
    assert dec_seq == 1 and seq_len % MOBA_BLOCK == 0 and past_len % MOBA_BLOCK == 0
    assert seq_len // MOBA_BLOCK <= GATE_SLOTS
    tm = 512 if seq_len % 512 == 0 else MOBA_BLOCK
    pages_per_step = 8 if n_pages % 8 == 0 else 2

    w_in_b, w_out_b = w_in.astype(BF16), w_out.astype(BF16)
    w_up_b, w_down_b = w_up.astype(BF16), w_down.astype(BF16)

    n_c = batch + n_seq
    c_rows = ((n_c + 7) // 8) * 8
    c_all = jnp.pad(jnp.concatenate([c_prompt, c_sample], axis=0), ((0, c_rows - n_c), (0, 0)))
    mod = _ada_call(c_all, w_ada, b_ada)
    lb = _lb_call(lb_param)
    rpe_flat = rpe_table.T.reshape(-1)
    rpe_pad = jnp.pad(rpe_table, ((0, 0), (0, LANES - ATT_HEADS)))
    bias_tiles = _bias_tile_call(rpe_flat)
    dec_bias = _dec_bias_call(rpe_pad, past_len)
    cache_k4 = cache_k.reshape(cache_k.shape[0], cache_k.shape[1], PAGE_SIZE, ATT_WIDTH)
    cache_v4 = cache_v.reshape(cache_v.shape[0], cache_v.shape[1], PAGE_SIZE, ATT_WIDTH)

    xp = x_prompt.reshape(batch * seq_len, D_MODEL)
    xs = x_sample.reshape(n_seq, D_MODEL)
    k_p, v_p, s_p, k_s, v_s, s_s = [], [], [], [], [], []
    for l in range(DEPTH):
        w = _layer_weights(w_in_b, w_out_b, w_up_b, w_down_b, l)
        row = lambda a: a[l].reshape(1, -1)
        gains = (row(g_post_mix), row(g_pre_ffn), row(g_post_ffn))
        lb_l, gon_l = row(lb), row(g_onorm)

        mp = mod[l, :, :batch].reshape(6, batch, 1, D_MODEL)
        qT, k, kaug, kmean, v, vT, zhg = _in_proj_call(xp, row(g_pre_mix), mp[1], mp[0], w, seq_len, tm)
        qaug = _gate_call(rpe_flat, kmean.reshape(batch, seq_len // MOBA_BLOCK, -1), qT, batch, seq_len)
        oT = _att_call(qaug, kaug, vT, bias_tiles, batch, seq_len)
        o_hg, sT = _hg_call(zhg, lb_l, gon_l, batch, seq_len, MOBA_BLOCK)
        xp = _post_call(xp, oT.T, o_hg, (mp[2], mp[4], mp[3], mp[5]), gains, w, seq_len, tm)
        k_p.append(k)
        v_p.append(v)
        s_p.append(jnp.swapaxes(sT, -1, -2))

        ms = mod[l, :, batch:n_c]
        z = _dec_in_call(xs, row(g_pre_mix), ms[1], ms[0], w["w_in"])
        a = ATT_WIDTH
        part = lambda i: z[:, i * a:(i + 1) * a].reshape(n_seq, 1, a)
        o_att = _dec_att_call(page_table, part(0), part(1), part(2), dec_bias, rpe_pad[0:1], cache_k4,
                              cache_v4, l, pages_per_step)
        o_hgs, s_new = _dec_hg_call(part(3), part(4), part(5), part(6), lb_l, gon_l, state_hgrn[l])
        ms4 = ms.reshape(6, 1, n_seq, D_MODEL)
        xs = _post_call(xs, o_att.reshape(n_seq, a), o_hgs.reshape(n_seq, a),
                        (ms4[2], ms4[4], ms4[3], ms4[5]), gains, w, n_seq, n_seq)
        k_s.append(z[:, a:2 * a])
        v_s.append(z[:, 2 * a:3 * a])
        s_s.append(s_new)

    hd = (ATT_HEADS, ATT_HEAD_DIM)
    return (xp.reshape(batch, seq_len, D_MODEL),
            xs.reshape(n_seq, 1, D_MODEL),
            jnp.stack(k_p).reshape(DEPTH, batch, seq_len, *hd),
            jnp.stack(v_p).reshape(DEPTH, batch, seq_len, *hd),
            jnp.stack(s_p),
            jnp.stack(k_s).reshape(DEPTH, n_seq, 1, *hd),
            jnp.stack(v_s).reshape(DEPTH, n_seq, 1, *hd),
            jnp.stack(s_s))
```

```python
import functools
import math

import numpy as np
import jax
import jax.numpy as jnp
from jax import lax
from jax.experimental import pallas as pl
from jax.experimental.pallas import tpu as pltpu

F32 = jnp.float32
BF16 = jnp.bfloat16

D_MODEL = 1024
DEPTH = 4
ATT_HEADS = 8
ATT_HEAD_DIM = 64
ATT_WIDTH = ATT_HEADS * ATT_HEAD_DIM
MOBA_BLOCK = 256
MOBA_TOPK = 3
PAGE_SIZE = 128
RPE_BUCKETS = 32
RPE_MAX_DIST = 128
HG_HEADS = 4
HG_DIM = 128
HG_WIDTH = HG_HEADS * HG_DIM
HG_SUB = 16
D_FF = 4 * D_MODEL
EPS = 1e-6

LANES = 128
HEAD_PAD = 128
GATE_SLOTS = 32
LOG2E = 1.4426950408889634
Q_SCALE = ATT_HEAD_DIM ** -0.5 * LOG2E
NEG = -30000.0
VMEM_LIMIT = 56 * 1024 * 1024

NT = (((1,), (1,)), ((), ()))


def _bucket_starts():
    max_exact = RPE_BUCKETS // 2
    d = np.arange(0, RPE_MAX_DIST + 1)
    dd = np.maximum(d, max_exact).astype(np.float32)
    large = max_exact + (np.log(dd / np.float32(max_exact)) / np.float32(math.log(RPE_MAX_DIST / max_exact))
                         * np.float32(RPE_BUCKETS - max_exact)).astype(np.int32)
    large = np.minimum(large, RPE_BUCKETS - 1)
    b = np.where(d < max_exact, d, large)
    return tuple(int(np.argmax(b >= k)) for k in range(RPE_BUCKETS))


BUCKET_STARTS = _bucket_starts()


def _params(sem):
    return pltpu.CompilerParams(dimension_semantics=sem, vmem_limit_bytes=VMEM_LIMIT)


def _rms(x, g):
    return x * lax.rsqrt(jnp.mean(x * x, axis=-1, keepdims=True) + EPS) * g


def _dot(a, b):
    return jnp.dot(a, b, preferred_element_type=F32)


def _split3(x):
    hi = x.astype(BF16)
    r = x - hi.astype(F32)
    mid = r.astype(BF16)
    lo = (r - mid.astype(F32)).astype(BF16)
    return hi, mid, lo


def _dot_exact_lhs(a, x):
    hi, mid, lo = _split3(x)
    return _dot(a, hi) + _dot(a, mid) + _dot(a, lo)


def _dot_exact_rhs(x, a):
    hi, mid, lo = _split3(x)
    return _dot(hi, a) + _dot(mid, a) + _dot(lo, a)


def _ada_kernel(c_ref, w_ref, b_ref, o_ref):
    c = c_ref[...]
    act = (c / (1.0 + jnp.exp(-c))).astype(BF16)
    o_ref[0, 0] = _dot(act, w_ref[0].astype(BF16)) + b_ref[0, 0]


def _ada_call(c_all, w_ada, b_ada):
    rows = c_all.shape[0]
    return pl.pallas_call(
        _ada_kernel,
        grid=(DEPTH, 6),
        in_specs=[pl.BlockSpec((rows, D_MODEL), lambda l, j: (0, 0)),
                  pl.BlockSpec((1, D_MODEL, D_MODEL), lambda l, j: (l, 0, j)),
                  pl.BlockSpec((1, 1, 1, D_MODEL), lambda l, j: (l, j, 0, 0))],
        out_specs=pl.BlockSpec((1, 1, rows, D_MODEL), lambda l, j: (l, j, 0, 0)),
        out_shape=jax.ShapeDtypeStruct((DEPTH, 6, rows, D_MODEL), F32),
        compiler_params=_params(("arbitrary", "arbitrary")),
        name="ada_mod",
    )(c_all, w_ada, b_ada.reshape(DEPTH, 6, 1, D_MODEL))


def _lb_kernel(p_ref, o_ref):
    p = p_ref[...]
    e = jnp.exp(p - jnp.max(p, axis=0, keepdims=True))
    sm = e / jnp.sum(e, axis=0, keepdims=True)
    acc = jnp.zeros((1, HG_WIDTH), F32)
    for l in range(DEPTH):
        o_ref[l:l + 1, :] = acc
        if l + 1 < DEPTH:
            acc = acc + sm[l + 1:l + 2, :]


def _lb_call(lb_param):
    return pl.pallas_call(
        _lb_kernel,
        out_shape=jax.ShapeDtypeStruct((DEPTH, HG_WIDTH), F32),
        name="hgrn_lower_bounds",
    )(lb_param)


def _bias_lookup(d, table):
    val = table(RPE_BUCKETS - 1)
    for b in range(RPE_BUCKETS - 2, -1, -1):
        val = jnp.where(d < BUCKET_STARTS[b + 1], table(b), val)
    return val * LOG2E


def _bias_tile_kernel(rpe_ref, o_ref):
    h = pl.program_id(0)
    kk = lax.broadcasted_iota(jnp.int32, (MOBA_BLOCK, MOBA_BLOCK), 0)
    qq = lax.broadcasted_iota(jnp.int32, (MOBA_BLOCK, MOBA_BLOCK), 1)
    d = qq - kk
    table = lambda b: rpe_ref[h * RPE_BUCKETS + b]
    o_ref[0, 0] = jnp.where(d >= 0, _bias_lookup(d, table), NEG)
    o_ref[0, 1] = _bias_lookup(d + MOBA_BLOCK, table)


def _bias_tile_call(rpe_flat):
    return pl.pallas_call(
        _bias_tile_kernel,
        grid=(ATT_HEADS,),
        in_specs=[pl.BlockSpec(memory_space=pltpu.SMEM)],
        out_specs=pl.BlockSpec((1, 2, MOBA_BLOCK, MOBA_BLOCK), lambda h: (h, 0, 0, 0)),
        out_shape=jax.ShapeDtypeStruct((ATT_HEADS, 2, MOBA_BLOCK, MOBA_BLOCK), F32),
        compiler_params=_params(("arbitrary",)),
        name="rpe_bias_tiles",
    )(rpe_flat)


def _dec_bias_kernel(rpe_ref, o_ref):
    assert BUCKET_STARTS[-1] <= PAGE_SIZE + 1
    d = PAGE_SIZE - lax.broadcasted_iota(jnp.int32, (PAGE_SIZE, ATT_HEADS, LANES), 0)
    table = lambda b: rpe_ref[:, b:b + 1][None]
    o_ref[...] = _bias_lookup(d, table)


def _dec_bias_call(rpe_heads):
    return pl.pallas_call(
        _dec_bias_kernel,
        out_shape=jax.ShapeDtypeStruct((PAGE_SIZE, ATT_HEADS, LANES), F32),
        name="rpe_bias_decode",
    )(rpe_heads)


def _in_proj_kernel(x_ref, g_ref, sc_ref, sh_ref, wqT_ref, wkp_ref, wv_ref, wvT_ref, whg_ref,
                    qT_ref, k_ref, kaug_ref, kmean_ref, v_ref, vT_ref, zhg_ref, *, blocks_per_seq):
    x = x_ref[...]
    tm = x.shape[0]
    h = (_rms(x, g_ref[...]) * (1.0 + sc_ref[0]) + sh_ref[0]).astype(BF16)
    qT = lax.dot_general(wqT_ref[...], h, NT, preferred_element_type=F32)
    qT_ref[...] = (qT * Q_SCALE).astype(BF16)
    vT_ref[...] = lax.dot_general(wvT_ref[...], h, NT, preferred_element_type=F32).astype(BF16)
    zhg_ref[...] = _dot(h, whg_ref[...])
    v = _dot(h, wv_ref[...])
    kp = _dot(h, wkp_ref[...])
    for hd in range(ATT_HEADS):
        k_ref[:, hd, :] = kp[:, hd * HEAD_PAD:hd * HEAD_PAD + ATT_HEAD_DIM]
        v_ref[:, hd, :] = v[:, hd * ATT_HEAD_DIM:(hd + 1) * ATT_HEAD_DIM]
    nb = tm // MOBA_BLOCK
    for r in range(nb):
        kmean_ref[r] = jnp.mean(kp[r * MOBA_BLOCK:(r + 1) * MOBA_BLOCK], axis=0, keepdims=True)
    row = lax.broadcasted_iota(jnp.int32, kp.shape, 0)
    lane = lax.broadcasted_iota(jnp.int32, kp.shape, 1) % HEAD_PAD
    blk = (pl.program_id(0) * nb + row // MOBA_BLOCK) % blocks_per_seq
    onehot = (lane == blk + ATT_HEAD_DIM) | (lane == blk + ATT_HEAD_DIM + GATE_SLOTS)
    kaug_ref[...] = jnp.where(onehot, 1.0, kp).astype(BF16)


def _in_proj_call(x2, g, sc, sh, w, seq_len, tm):
    t = x2.shape[0]
    tiles_per_seq = seq_len // tm
    nbt = tm // MOBA_BLOCK
    hp = ATT_HEADS * HEAD_PAD
    row = lambda i: (i, 0)
    col = lambda i: (0, i)
    fixed = lambda i: (0, 0)
    mod = lambda i: (i // tiles_per_seq, 0, 0)
    wspec = lambda a: pl.BlockSpec(a.shape, fixed)
    return pl.pallas_call(
        functools.partial(_in_proj_kernel, blocks_per_seq=seq_len // MOBA_BLOCK),
        grid=(t // tm,),
        in_specs=[pl.BlockSpec((tm, D_MODEL), row),
                  pl.BlockSpec((1, D_MODEL), fixed),
                  pl.BlockSpec((1, 1, D_MODEL), mod),
                  pl.BlockSpec((1, 1, D_MODEL), mod),
                  wspec(w["wqT"]), wspec(w["wkp"]), wspec(w["wv"]), wspec(w["wvT"]), wspec(w["whg"])],
        out_specs=[pl.BlockSpec((ATT_WIDTH, tm), col),
                   pl.BlockSpec((tm, ATT_HEADS, ATT_HEAD_DIM), lambda i: (i, 0, 0)),
                   pl.BlockSpec((tm, hp), row),
                   pl.BlockSpec((nbt, 1, hp), lambda i: (i, 0, 0)),
                   pl.BlockSpec((tm, ATT_HEADS, ATT_HEAD_DIM), lambda i: (i, 0, 0)),
                   pl.BlockSpec((ATT_WIDTH, tm), col),
                   pl.BlockSpec((tm, 4 * HG_WIDTH), row)],
        out_shape=[jax.ShapeDtypeStruct((ATT_WIDTH, t), BF16),
                   jax.ShapeDtypeStruct((t, ATT_HEADS, ATT_HEAD_DIM), F32),
                   jax.ShapeDtypeStruct((t, hp), BF16),
                   jax.ShapeDtypeStruct((t // MOBA_BLOCK, 1, hp), F32),
                   jax.ShapeDtypeStruct((t, ATT_HEADS, ATT_HEAD_DIM), F32),
                   jax.ShapeDtypeStruct((ATT_WIDTH, t), BF16),
                   jax.ShapeDtypeStruct((t, 4 * HG_WIDTH), F32)],
        compiler_params=_params(("arbitrary",)),
        name="in_proj",
    )(x2, g, sc, sh, w["wqT"], w["wkp"], w["wv"], w["wvT"], w["whg"])


def _gate_kernel(rpe_ref, kmean_ref, qT_ref, o_ref, *, nbs):
    own = pl.program_id(1)
    tq = qT_ref.shape[1]
    n = lax.broadcasted_iota(jnp.int32, (nbs, tq), 0)
    far = n <= own - 2
    for h in range(ATT_HEADS):
        q = qT_ref[h * ATT_HEAD_DIM:(h + 1) * ATT_HEAD_DIM, :]
        km = kmean_ref[0][:, h * HEAD_PAD:h * HEAD_PAD + ATT_HEAD_DIM].astype(BF16)
        gm = jnp.where(n < own, _dot(km, q), -jnp.inf)
        sel = jnp.zeros((nbs, tq), jnp.bool_)
        for j in range(MOBA_TOPK):
            mx = jnp.max(gm, axis=0, keepdims=True)
            idx = jnp.min(jnp.where(gm == mx, n, nbs), axis=0, keepdims=True)
            hit = n == idx
            sel = sel | (hit & (j < own))
            gm = jnp.where(hit, -jnp.inf, gm)
        c = jnp.full((nbs, tq), rpe_ref[h * RPE_BUCKETS + RPE_BUCKETS - 1] * LOG2E, F32)
        c_hi = c.astype(BF16).astype(F32)
        c_lo = c - c_hi
        p_hi = jnp.where(sel, jnp.where(far, c_hi, 0.0), NEG)
        p_hi = jnp.where(n >= own, 0.0, p_hi)
        p_lo = jnp.where(sel & far, c_lo, 0.0)
        base = h * HEAD_PAD
        o_ref[base:base + ATT_HEAD_DIM, :] = q
        o_ref[base + ATT_HEAD_DIM:base + HEAD_PAD, :] = jnp.zeros((HEAD_PAD - ATT_HEAD_DIM, tq), BF16)
        o_ref[base + ATT_HEAD_DIM:base + ATT_HEAD_DIM + nbs, :] = p_hi.astype(BF16)
        o_ref[base + ATT_HEAD_DIM + GATE_SLOTS:base + ATT_HEAD_DIM + GATE_SLOTS + nbs, :] = p_lo.astype(BF16)


def _gate_call(rpe_flat, kmean, qT, batch, seq_len):
    nbs = seq_len // MOBA_BLOCK
    hp = ATT_HEADS * HEAD_PAD
    t = qT.shape[1]
    return pl.pallas_call(
        functools.partial(_gate_kernel, nbs=nbs),
        grid=(batch, nbs),
        in_specs=[pl.BlockSpec(memory_space=pltpu.SMEM),
                  pl.BlockSpec((1, nbs, hp), lambda b, i: (b, 0, 0)),
                  pl.BlockSpec((ATT_WIDTH, MOBA_BLOCK), lambda b, i: (0, b * nbs + i))],
        out_specs=pl.BlockSpec((hp, MOBA_BLOCK), lambda b, i: (0, b * nbs + i)),
        out_shape=jax.ShapeDtypeStruct((hp, t), BF16),
        compiler_params=_params(("arbitrary", "arbitrary")),
        name="moba_gate",
    )(rpe_flat, kmean, qT)


def _att_kernel(q_ref, k_ref, vT_ref, bias_ref, o_ref, s_scr, p_scr, m_scr, l_scr, a_scr, acc_scr, *, heads):
    qi = pl.program_id(2)
    tq = q_ref.shape[1]

    def block(n, which):
        start = pl.multiple_of(n * MOBA_BLOCK, MOBA_BLOCK)
        for h in range(heads):
            q = q_ref[h * HEAD_PAD:(h + 1) * HEAD_PAD, :]
            s = _dot(k_ref[pl.ds(start, MOBA_BLOCK), h * HEAD_PAD:(h + 1) * HEAD_PAD], q)
            if which is not None:
                s = s + bias_ref[h, which]
            s_scr[h] = s
        for h in range(heads):
            s = s_scr[h]
            m = m_scr[h]
            m_new = jnp.maximum(m, jnp.max(s, axis=0, keepdims=True))
            alpha = jnp.exp2(m - m_new)
            p = jnp.exp2(s - m_new)
            l_scr[h] = alpha * l_scr[h] + jnp.sum(p, axis=0, keepdims=True)
            m_scr[h] = m_new
            a_scr[h] = alpha
            p_scr[h] = p.astype(BF16)
        for h in range(heads):
            v = vT_ref[h * ATT_HEAD_DIM:(h + 1) * ATT_HEAD_DIM, pl.ds(start, MOBA_BLOCK)]
            acc_scr[h] = a_scr[h] * acc_scr[h] + _dot(v, p_scr[h])

    m_scr[...] = jnp.full(m_scr.shape, -jnp.inf, F32)
    l_scr[...] = jnp.zeros(l_scr.shape, F32)
    acc_scr[...] = jnp.zeros(acc_scr.shape, F32)
    block(qi, 0)

    @pl.when(qi >= 1)
    def _():
        block(qi - 1, 1)

    @pl.loop(0, jnp.maximum(qi - 1, 0))
    def _(n):
        block(n, None)

    for h in range(heads):
        o_ref[h * ATT_HEAD_DIM:(h + 1) * ATT_HEAD_DIM, :] = (acc_scr[h] / l_scr[h]).astype(BF16)


ATT_HEADS_PER_STEP = 4


def _att_call(qaug, kaug, vT, bias_tiles, batch, seq_len):
    nq = seq_len // MOBA_BLOCK
    t = qaug.shape[1]
    hb = ATT_HEADS_PER_STEP
    return pl.pallas_call(
        functools.partial(_att_kernel, heads=hb),
        grid=(batch, ATT_HEADS // hb, nq),
        in_specs=[pl.BlockSpec((hb * HEAD_PAD, MOBA_BLOCK), lambda b, h, i: (h, b * nq + i)),
                  pl.BlockSpec((seq_len, hb * HEAD_PAD), lambda b, h, i: (b, h)),
                  pl.BlockSpec((hb * ATT_HEAD_DIM, seq_len), lambda b, h, i: (h, b)),
                  pl.BlockSpec((hb, 2, MOBA_BLOCK, MOBA_BLOCK), lambda b, h, i: (h, 0, 0, 0))],
        out_specs=pl.BlockSpec((hb * ATT_HEAD_DIM, MOBA_BLOCK), lambda b, h, i: (h, b * nq + i)),
        out_shape=jax.ShapeDtypeStruct((ATT_WIDTH, t), BF16),
        scratch_shapes=[pltpu.VMEM((hb, MOBA_BLOCK, MOBA_BLOCK), F32),
                        pltpu.VMEM((hb, MOBA_BLOCK, MOBA_BLOCK), BF16),
                        pltpu.VMEM((hb, 1, MOBA_BLOCK), F32),
                        pltpu.VMEM((hb, 1, MOBA_BLOCK), F32),
                        pltpu.VMEM((hb, 1, MOBA_BLOCK), F32),
                        pltpu.VMEM((hb, ATT_HEAD_DIM, MOBA_BLOCK), F32)],
        compiler_params=_params(("arbitrary", "arbitrary", "arbitrary")),
        name="moba_attention",
    )(qaug, kaug, vT, bias_tiles)


def _log_forget(fpre, lb):
    log_sig = jnp.minimum(fpre, 0.0) - jnp.log1p(jnp.exp(-jnp.abs(fpre)))
    a1 = jnp.log(lb)
    a2 = jnp.log1p(-lb) + log_sig
    return jnp.maximum(a1, a2) + jnp.log1p(jnp.exp(-jnp.abs(a1 - a2)))


def _hg_kernel(qh_ref, fh_ref, ih_ref, gh_ref, lb_ref, gon_ref, o_ref, sT_ref):
    @pl.when(pl.program_id(2) == 0)
    def _():
        sT_ref[...] = jnp.zeros_like(sT_ref)

    ts = qh_ref.shape[0]
    n_sub = ts // HG_SUB
    lb = lb_ref[...]
    fpre = fh_ref[...]
    qh = qh_ref[...]
    v = ih_ref[...]
    logf = _log_forget(fpre, lb)
    kk = (1.0 - lb) / (1.0 + jnp.exp(fpre))
    q = qh / (1.0 + jnp.exp(-qh))

    r = lax.broadcasted_iota(jnp.int32, (ts, ts), 0)
    c = lax.broadcasted_iota(jnp.int32, (ts, ts), 1)
    same = (r // HG_SUB) == (c // HG_SUB)
    tri = jnp.where(same & (c <= r), 1.0, 0.0).astype(BF16)
    ones = jnp.where(same, 1.0, 0.0).astype(BF16)
    b = _dot_exact_lhs(tri, logf)
    b_end = _dot_exact_lhs(ones, logf)
    q_dec = (q * jnp.exp(b)).astype(BF16)
    k_dec = (kk * jnp.exp(b_end - b)).astype(BF16)
    decay = jnp.exp(b_end)

    tmod = lax.broadcasted_iota(jnp.int32, (ts, HG_DIM), 0) % HG_SUB
    o = jnp.zeros((ts, HG_DIM), F32)
    for dlt in range(HG_SUB):
        if dlt == 0:
            bs, ks, vs = b, kk, v
        else:
            bs = pltpu.roll(b, dlt, 0)
            ks = pltpu.roll(kk, dlt, 0)
            vs = pltpu.roll(v, dlt, 0)
        e = jnp.exp(jnp.where(tmod >= dlt, b - bs, -1e30))
        a = jnp.sum(q * ks * e, axis=-1, keepdims=True)
        o = o + a * vs

    vT = v.T
    lane_sub = lax.broadcasted_iota(jnp.int32, (HG_DIM, ts), 1) // HG_SUB
    sT = sT_ref[0, 0]
    inter = []
    for j in range(n_sub):
        rows = slice(j * HG_SUB, (j + 1) * HG_SUB)
        inter.append(lax.dot_general(q_dec[rows], sT.astype(BF16), NT, preferred_element_type=F32))
        vT_j = jnp.where(lane_sub == j, vT, 0.0).astype(BF16)
        sT = sT * decay[j * HG_SUB:j * HG_SUB + 1, :] + _dot(vT_j, k_dec)
    sT_ref[0, 0] = sT
    o = o + jnp.concatenate(inter, axis=0)

    gh = gh_ref[...]
    o = _rms(o, gon_ref[...]) * (gh / (1.0 + jnp.exp(-gh)))
    o_ref[...] = o.astype(BF16)


def _hg_call(zhg, lb_l, gon_l, batch, seq_len, ts):
    t = zhg.shape[0]
    nt = seq_len // ts
    part = lambda p: pl.BlockSpec((ts, HG_DIM), lambda b, h, i: (b * nt + i, p * HG_HEADS + h))
    vec = pl.BlockSpec((1, HG_DIM), lambda b, h, i: (0, h))
    return pl.pallas_call(
        _hg_kernel,
        grid=(batch, HG_HEADS, nt),
        in_specs=[part(0), part(1), part(2), part(3), vec, vec],
        out_specs=[pl.BlockSpec((ts, HG_DIM), lambda b, h, i: (b * nt + i, h)),
                   pl.BlockSpec((1, 1, HG_DIM, HG_DIM), lambda b, h, i: (b, h, 0, 0))],
        out_shape=[jax.ShapeDtypeStruct((t, HG_WIDTH), BF16),
                   jax.ShapeDtypeStruct((batch, HG_HEADS, HG_DIM, HG_DIM), F32)],
        compiler_params=_params(("arbitrary", "arbitrary", "arbitrary")),
        name="hgrn2_prompt",
    )(zhg, zhg, zhg, zhg, lb_l, gon_l)


def _post_kernel(x_ref, oa_ref, oh_ref, ga1_ref, sc2_ref, sh2_ref, ga2_ref, gpm_ref, gpf_ref, gqf_ref,
                 woa_ref, woh_ref, wup_ref, wdn_ref, out_ref, *, ff_chunk):
    x = x_ref[...]
    y = _dot(oa_ref[...], woa_ref[...]) + _dot(oh_ref[...], woh_ref[...])
    x1 = x + ga1_ref[0] * _rms(y, gpm_ref[...])
    h2 = (_rms(x1, gpf_ref[...]) * (1.0 + sc2_ref[0]) + sh2_ref[0]).astype(BF16)
    m = jnp.zeros(x.shape, F32)
    for c in range(D_FF // ff_chunk):
        cols = slice(c * ff_chunk, (c + 1) * ff_chunk)
        u = jnp.square(jnp.maximum(_dot(h2, wup_ref[:, cols]), 0.0)).astype(BF16)
        m = m + _dot(u, wdn_ref[cols, :])
    out_ref[...] = x1 + ga2_ref[0] * _rms(m, gqf_ref[...])


def _post_call(x2, oa, oh, mods, gains, w, rows_per_mod, tm):
    t = x2.shape[0]
    row = lambda i: (i, 0)
    fixed = lambda i: (0, 0)
    mod_rows = mods[0].shape[1]
    mod = pl.BlockSpec((1, mod_rows, D_MODEL), lambda i: ((i * tm) // rows_per_mod, 0, 0))
    gain = pl.BlockSpec((1, D_MODEL), fixed)
    wspec = lambda a: pl.BlockSpec(a.shape, fixed)
    return pl.pallas_call(
        functools.partial(_post_kernel, ff_chunk=1024),
        grid=(t // tm,),
        in_specs=[pl.BlockSpec((tm, D_MODEL), row),
                  pl.BlockSpec((tm, ATT_WIDTH), row),
                  pl.BlockSpec((tm, HG_WIDTH), row),
                  mod, mod, mod, mod, gain, gain, gain,
                  wspec(w["woa"]), wspec(w["woh"]), wspec(w["wup"]), wspec(w["wdn"])],
        out_specs=pl.BlockSpec((tm, D_MODEL), row),
        out_shape=jax.ShapeDtypeStruct((t, D_MODEL), F32),
        compiler_params=_params(("arbitrary",)),
        name="out_proj_mlp",
    )(x2, oa, oh, *mods, *gains, w["woa"], w["woh"], w["wup"], w["wdn"])


def _dec_in_kernel(x_ref, g_ref, sc_ref, sh_ref, w_ref, z_ref):
    h = (_rms(x_ref[...], g_ref[...]) * (1.0 + sc_ref[...]) + sh_ref[...]).astype(BF16)
    z_ref[...] = _dot(h, w_ref[...])


def _dec_in_call(x2, g, sc, sh, w_in_l):
    rows = x2.shape[0]
    n_in = w_in_l.shape[1]
    tn = 512
    full = pl.BlockSpec((rows, D_MODEL), lambda j: (0, 0))
    return pl.pallas_call(
        _dec_in_kernel,
        grid=(n_in // tn,),
        in_specs=[full, pl.BlockSpec((1, D_MODEL), lambda j: (0, 0)), full, full,
                  pl.BlockSpec((D_MODEL, tn), lambda j: (0, j))],
        out_specs=pl.BlockSpec((rows, tn), lambda j: (0, j)),
        out_shape=jax.ShapeDtypeStruct((rows, n_in), F32),
        compiler_params=_params(("arbitrary",)),
        name="in_proj_sample",
    )(x2, g, sc, sh, w_in_l)


def _dec_att_kernel(pt_ref, q_ref, kn_ref, vn_ref, bias_ref, rpe0_ref, *refs, pages_per_step, n_blocks):
    k_refs = refs[:pages_per_step]
    v_refs = refs[pages_per_step:2 * pages_per_step]
    o_ref = refs[2 * pages_per_step]
    m_s, l_s, g_s, acc_s, qbd_s = refs[2 * pages_per_step + 1:]
    j = pl.program_id(1)
    bps = pages_per_step // 2

    lane = lax.broadcasted_iota(jnp.int32, (LANES, ATT_WIDTH), 1)
    hrow = lax.broadcasted_iota(jnp.int32, (LANES, ATT_WIDTH), 0)
    expand = jnp.where(lane // ATT_HEAD_DIM == hrow, 1.0, 0.0).astype(BF16)
    qs = q_ref[0] * Q_SCALE

    def collapse():
        d = lax.broadcasted_iota(jnp.int32, (ATT_WIDTH, LANES), 0)
        hcol = lax.broadcasted_iota(jnp.int32, (ATT_WIDTH, LANES), 1)
        return jnp.where(d // ATT_HEAD_DIM == hcol, 1.0, 0.0)

    @pl.when(j == 0)
    def _():
        m_s[...] = jnp.zeros_like(m_s)
        l_s[...] = jnp.zeros_like(l_s)
        g_s[...] = jnp.zeros_like(g_s)
        acc_s[...] = jnp.zeros_like(acc_s)
        r = lax.broadcasted_iota(jnp.int32, (ATT_WIDTH, ATT_WIDTH), 0)
        c = lax.broadcasted_iota(jnp.int32, (ATT_WIDTH, ATT_WIDTH), 1)
        q_col = jnp.sum(jnp.where(r == c, jnp.broadcast_to(qs, (ATT_WIDTH, ATT_WIDTH)), 0.0),
                        axis=1, keepdims=True)
        qbd_s[...] = (collapse() * q_col).astype(BF16)

    q_bd = qbd_s[...]
    for blk in range(bps):
        s_raw = [_dot(k_refs[2 * blk + hf][0, 0].astype(BF16), q_bd) for hf in range(2)]
        g = jnp.sum(s_raw[0], axis=0, keepdims=True) + jnp.sum(s_raw[1], axis=0, keepdims=True)
        s = [s_raw[hf] + bias_ref[(2 * blk + hf) * PAGE_SIZE:(2 * blk + hf + 1) * PAGE_SIZE, :]
             for hf in range(2)]
        m = jnp.maximum(jnp.max(s[0], axis=0, keepdims=True), jnp.max(s[1], axis=0, keepdims=True))
        l = jnp.zeros((1, LANES), F32)
        acc = jnp.zeros((1, ATT_WIDTH), F32)
        for hf in range(2):
            p = jnp.exp2(s[hf] - m)
            l = l + jnp.sum(p, axis=0, keepdims=True)
            p_wide = _dot(p.astype(BF16), expand)
            acc = acc + jnp.sum(p_wide * v_refs[2 * blk + hf][0, 0], axis=0, keepdims=True)
        row = j * bps + blk
        m_s[pl.ds(row, 1), :] = m
        l_s[pl.ds(row, 1), :] = l
        g_s[pl.ds(row, 1), :] = g
        acc_s[pl.ds(row, 1), :] = acc

    @pl.when(j == pl.num_programs(1) - 1)
    def _():
        s_new = _dot_exact_rhs(qs * kn_ref[0], collapse().astype(BF16)) + rpe0_ref[...] * LOG2E
        m_s[n_blocks:n_blocks + 1, :] = s_new
        l_s[n_blocks:n_blocks + 1, :] = jnp.ones((1, LANES), F32)
        acc_s[n_blocks:n_blocks + 1, :] = vn_ref[0]
        rows = m_s.shape[0]
        n = lax.broadcasted_iota(jnp.int32, (rows, LANES), 0)
        gm = jnp.where(n < n_blocks, g_s[...], -jnp.inf)
        sel = n == n_blocks
        for _ in range(min(MOBA_TOPK, n_blocks)):
            mx = jnp.max(gm, axis=0, keepdims=True)
            idx = jnp.min(jnp.where(gm == mx, n, rows), axis=0, keepdims=True)
            hit = n == idx
            sel = sel | hit
            gm = jnp.where(hit, -jnp.inf, gm)
        mm = jnp.where(sel, m_s[...], -jnp.inf)
        w = jnp.exp2(mm - jnp.max(mm, axis=0, keepdims=True))
        den = jnp.sum(w * l_s[...], axis=0, keepdims=True)
        num = jnp.sum(_dot_exact_rhs(w, expand) * acc_s[...], axis=0, keepdims=True)
        o_ref[0] = (num / _dot_exact_rhs(den, expand)).astype(BF16)


def _dec_att_call(page_table, q3, kn3, vn3, dec_bias, rpe0, cache_k4, cache_v4, layer, pages_per_step):
    n_seq, n_pages = page_table.shape
    n_blocks = n_pages * PAGE_SIZE // MOBA_BLOCK
    steps = n_pages // pages_per_step
    rows = ((n_blocks + 1 + 7) // 8) * 8
    vec = pl.BlockSpec((1, 1, ATT_WIDTH), lambda b, j, pt: (b, 0, 0))

    def page_spec(p):
        return pl.BlockSpec((1, 1, PAGE_SIZE, ATT_WIDTH),
                            lambda b, j, pt: (layer, pt[b, j * pages_per_step + p], 0, 0))

    grid_spec = pltpu.PrefetchScalarGridSpec(
        num_scalar_prefetch=1,
        grid=(n_seq, steps),
        in_specs=[vec, vec, vec,
                  pl.BlockSpec((pages_per_step * PAGE_SIZE, LANES), lambda b, j, pt: (j, 0)),
                  pl.BlockSpec((1, LANES), lambda b, j, pt: (0, 0))]
                 + [page_spec(p) for p in range(pages_per_step)]
                 + [page_spec(p) for p in range(pages_per_step)],
        out_specs=pl.BlockSpec((1, 1, ATT_WIDTH), lambda b, j, pt: (b, 0, 0)),
        scratch_shapes=[pltpu.VMEM((rows, LANES), F32), pltpu.VMEM((rows, LANES), F32),
                        pltpu.VMEM((rows, LANES), F32), pltpu.VMEM((rows, ATT_WIDTH), F32),
                        pltpu.VMEM((ATT_WIDTH, LANES), BF16)],
    )
    return pl.pallas_call(
        functools.partial(_dec_att_kernel, pages_per_step=pages_per_step, n_blocks=n_blocks),
        grid_spec=grid_spec,
        out_shape=jax.ShapeDtypeStruct((n_seq, 1, ATT_WIDTH), BF16),
        compiler_params=_params(("arbitrary", "arbitrary")),
        name="moba_attention_sample",
    )(page_table, q3, kn3, vn3, dec_bias, rpe0,
      *([cache_k4] * pages_per_step), *([cache_v4] * pages_per_step))


def _dec_att_kernel(pt_ref, q_ref, kn_ref, vn_ref, bias_ref, rpe_ref, *refs, pages_per_step, n_blocks):
    k_refs = refs[:pages_per_step]
    v_refs = refs[pages_per_step:2 * pages_per_step]
    o_ref = refs[2 * pages_per_step]
    m_s, l_s, g_s, acc_s = refs[2 * pages_per_step + 1:]
    j = pl.program_id(1)
    last_step = j == pl.num_programs(1) - 1
    bps = pages_per_step // 2
    q = (q_ref[0] * Q_SCALE)[None]
    c_far = (rpe_ref[:, RPE_BUCKETS - 1:RPE_BUCKETS] * LOG2E)[None]
    wide = (1, ATT_HEADS, LANES)
    ones = jnp.ones((ATT_HEAD_DIM, LANES), BF16)

    def scores(k_page):
        kq = (k_page * q).reshape(PAGE_SIZE * ATT_HEADS, ATT_HEAD_DIM).astype(BF16)
        return _dot(kq, ones).reshape(PAGE_SIZE, ATT_HEADS, LANES)

    @pl.when(j == 0)
    def _():
        m_s[...] = jnp.zeros_like(m_s)
        l_s[...] = jnp.zeros_like(l_s)
        g_s[...] = jnp.zeros_like(g_s)
        acc_s[...] = jnp.zeros_like(acc_s)

    for blk in range(bps):
        s = []
        g = jnp.zeros(wide, F32)
        for hf in range(2):
            p_idx = 2 * blk + hf
            s_raw = scores(k_refs[p_idx][0, 0])
            g = g + jnp.sum(s_raw, axis=0, keepdims=True)
            if p_idx == pages_per_step - 1:
                s.append(s_raw + jnp.where(last_step, bias_ref[...], c_far))
            else:
                s.append(s_raw + c_far)
        m = jnp.maximum(jnp.max(s[0], axis=0, keepdims=True), jnp.max(s[1], axis=0, keepdims=True))
        l = jnp.zeros(wide, F32)
        acc = jnp.zeros((1, ATT_HEADS, ATT_HEAD_DIM), F32)
        for hf in range(2):
            p = jnp.exp2(s[hf] - m)
            l = l + jnp.sum(p, axis=0, keepdims=True)
            acc = acc + jnp.sum(p[:, :, :ATT_HEAD_DIM] * v_refs[2 * blk + hf][0, 0], axis=0, keepdims=True)
        row = j * bps + blk
        m_s[pl.ds(row, 1)] = m
        l_s[pl.ds(row, 1)] = l
        g_s[pl.ds(row, 1)] = g
        acc_s[pl.ds(row, 1)] = acc

    @pl.when(last_step)
    def _():
        s_new = (jnp.sum(q * kn_ref[0][None], axis=-1, keepdims=True)
                 + (rpe_ref[:, 0:1] * LOG2E)[None])
        m_s[n_blocks:n_blocks + 1] = jnp.broadcast_to(s_new, wide)
        l_s[n_blocks:n_blocks + 1] = jnp.ones(wide, F32)
        acc_s[n_blocks:n_blocks + 1] = vn_ref[0][None]
        rows = m_s.shape[0]
        n = lax.broadcasted_iota(jnp.int32, m_s.shape, 0)
        gm = jnp.where(n < n_blocks, g_s[...], -jnp.inf)
        sel = n == n_blocks
        for _ in range(min(MOBA_TOPK, n_blocks)):
            mx = jnp.max(gm, axis=0, keepdims=True)
            idx = jnp.min(jnp.where(gm == mx, n, rows), axis=0, keepdims=True)
            hit = n == idx
            sel = sel | hit
            gm = jnp.where(hit, -jnp.inf, gm)
        mm = jnp.where(sel, m_s[...], -jnp.inf)
        w = jnp.exp2(mm - jnp.max(mm, axis=0, keepdims=True))
        den = jnp.sum(w * l_s[...], axis=0)
        num = jnp.sum(w[:, :, :ATT_HEAD_DIM] * acc_s[...], axis=0)
        o_ref[0] = (num / den[:, :ATT_HEAD_DIM]).astype(BF16)


def _dec_att_call(page_table, q3, kn3, vn3, dec_bias, rpe_heads, cache_k, cache_v, layer, pages_per_step):
    n_seq, n_pages = page_table.shape
    n_blocks = n_pages * PAGE_SIZE // MOBA_BLOCK
    steps = n_pages // pages_per_step
    rows = ((n_blocks + 1 + 7) // 8) * 8
    hd = (ATT_HEADS, ATT_HEAD_DIM)
    vec = pl.BlockSpec((1, *hd), lambda b, j, pt: (b, 0, 0))

    def page_spec(p):
        return pl.BlockSpec((1, 1, PAGE_SIZE, *hd),
                            lambda b, j, pt: (layer, pt[b, j * pages_per_step + p], 0, 0, 0))

    grid_spec = pltpu.PrefetchScalarGridSpec(
        num_scalar_prefetch=1,
        grid=(n_seq, steps),
        in_specs=[vec, vec, vec,
                  pl.BlockSpec((PAGE_SIZE, ATT_HEADS, LANES), lambda b, j, pt: (0, 0, 0)),
                  pl.BlockSpec((ATT_HEADS, RPE_BUCKETS), lambda b, j, pt: (0, 0))]
                 + [page_spec(p) for p in range(pages_per_step)]
                 + [page_spec(p) for p in range(pages_per_step)],
        out_specs=pl.BlockSpec((1, *hd), lambda b, j, pt: (b, 0, 0)),
        scratch_shapes=[pltpu.VMEM((rows, ATT_HEADS, LANES), F32), pltpu.VMEM((rows, ATT_HEADS, LANES), F32),
                        pltpu.VMEM((rows, ATT_HEADS, LANES), F32), pltpu.VMEM((rows, *hd), F32)],
    )
    return pl.pallas_call(
        functools.partial(_dec_att_kernel, pages_per_step=pages_per_step, n_blocks=n_blocks),
        grid_spec=grid_spec,
        out_shape=jax.ShapeDtypeStruct((n_seq, *hd), BF16),
        compiler_params=_params(("arbitrary", "arbitrary")),
        name="moba_attention_sample",
    )(page_table, q3, kn3, vn3, dec_bias, rpe_heads,
      *([cache_k] * pages_per_step), *([cache_v] * pages_per_step))


def _dec_hg_kernel(qh_ref, fh_ref, ih_ref, gh_ref, lb_ref, gon_ref, s_ref, o_ref, so_ref):
    r = lax.broadcasted_iota(jnp.int32, (HG_DIM, HG_DIM), 0)
    c = lax.broadcasted_iota(jnp.int32, (HG_DIM, HG_DIM), 1)
    eye = r == c

    def column(x):
        return jnp.sum(jnp.where(eye, jnp.broadcast_to(x, (HG_DIM, HG_DIM)), 0.0), axis=1, keepdims=True)

    for h in range(HG_HEADS):
        cols = slice(h * HG_DIM, (h + 1) * HG_DIM)
        lb = lb_ref[:, cols]
        fpre = fh_ref[0][:, cols]
        qh = qh_ref[0][:, cols]
        gh = gh_ref[0][:, cols]
        v = ih_ref[0][:, cols]
        f = jnp.exp(_log_forget(fpre, lb))
        kk = (1.0 - lb) / (1.0 + jnp.exp(fpre))
        q = qh / (1.0 + jnp.exp(-qh))
        s_new = column(f) * s_ref[0, h] + column(kk) * v
        so_ref[0, h] = s_new
        o = jnp.sum(column(q) * s_new, axis=0, keepdims=True)
        o = _rms(o, gon_ref[:, cols]) * (gh / (1.0 + jnp.exp(-gh)))
        o_ref[0, :, cols] = o.astype(BF16)


def _dec_hg_call(qh3, fh3, ih3, gh3, lb_l, gon_l, state_l):
    n_seq = qh3.shape[0]
    vec = pl.BlockSpec((1, 1, HG_WIDTH), lambda b: (b, 0, 0))
    par = pl.BlockSpec((1, HG_WIDTH), lambda b: (0, 0))
    st = pl.BlockSpec((1, HG_HEADS, HG_DIM, HG_DIM), lambda b: (b, 0, 0, 0))
    return pl.pallas_call(
        _dec_hg_kernel,
        grid=(n_seq,),
        in_specs=[vec, vec, vec, vec, par, par, st],
        out_specs=[vec, st],
        out_shape=[jax.ShapeDtypeStruct((n_seq, 1, HG_WIDTH), BF16),
                   jax.ShapeDtypeStruct(state_l.shape, F32)],
        compiler_params=_params(("arbitrary",)),
        name="hgrn2_sample",
    )(qh3, fh3, ih3, gh3, lb_l, gon_l, state_l)


def _layer_weights(w_in, w_out, w_up, w_down, l):
    a = ATT_WIDTH
    wi = w_in[l]
    wq, wk, wv, whg = wi[:, :a], wi[:, a:2 * a], wi[:, 2 * a:3 * a], wi[:, 3 * a:]
    wkp = jnp.pad(wk.reshape(D_MODEL, ATT_HEADS, ATT_HEAD_DIM),
                  ((0, 0), (0, 0), (0, HEAD_PAD - ATT_HEAD_DIM))).reshape(D_MODEL, ATT_HEADS * HEAD_PAD)
    return {"wqT": wq.T, "wk": wk, "wkp": wkp, "wv": wv, "wvT": wv.T, "whg": whg, "w_in": wi,
            "woa": w_out[l, :a], "woh": w_out[l, a:], "wup": w_up[l], "wdn": w_down[l]}


def kernel(x_prompt, x_sample, c_prompt, c_sample, cache_k, cache_v, state_hgrn, page_table, w_ada, b_ada,
           g_pre_mix, g_post_mix, g_pre_ffn, g_post_ffn, w_in, lb_param, g_onorm, w_out, w_up, w_down,
           rpe_table):
    batch, seq_len, _ = x_prompt.shape
    n_seq, dec_seq, _ = x_sample.shape
    n_pages = page_table.shape[1]
    past_len = n_pages * PAGE_SIZE
    assert dec_seq == 1 and seq_len % MOBA_BLOCK == 0 and past_len % MOBA_BLOCK == 0
    assert seq_len // MOBA_BLOCK <= GATE_SLOTS
    tm = 512 if seq_len % 512 == 0 else MOBA_BLOCK
    pages_per_step = 8 if n_pages % 8 == 0 else 2

    w_in_b, w_out_b = w_in.astype(BF16), w_out.astype(BF16)
    w_up_b, w_down_b = w_up.astype(BF16), w_down.astype(BF16)

    n_c = batch + n_seq
    c_rows = ((n_c + 7) // 8) * 8
    c_all = jnp.pad(jnp.concatenate([c_prompt, c_sample], axis=0), ((0, c_rows - n_c), (0, 0)))
    mod = _ada_call(c_all, w_ada, b_ada)
    lb = _lb_call(lb_param)
    rpe_flat = rpe_table.T.reshape(-1)
    rpe_heads = rpe_table.T
    bias_tiles = _bias_tile_call(rpe_flat)
    dec_bias = _dec_bias_call(rpe_heads)

    xp = x_prompt.reshape(batch * seq_len, D_MODEL)
    xs = x_sample.reshape(n_seq, D_MODEL)
    k_p, v_p, s_p, k_s, v_s, s_s = [], [], [], [], [], []
    for l in range(DEPTH):
        w = _layer_weights(w_in_b, w_out_b, w_up_b, w_down_b, l)
        row = lambda a: a[l].reshape(1, -1)
        gains = (row(g_post_mix), row(g_pre_ffn), row(g_post_ffn))
        lb_l, gon_l = row(lb), row(g_onorm)

        mp = mod[l, :, :batch].reshape(6, batch, 1, D_MODEL)
        qT, k, kaug, kmean, v, vT, zhg = _in_proj_call(xp, row(g_pre_mix), mp[1], mp[0], w, seq_len, tm)
        qaug = _gate_call(rpe_flat, kmean.reshape(batch, seq_len // MOBA_BLOCK, -1), qT, batch, seq_len)
        oT = _att_call(qaug, kaug, vT, bias_tiles, batch, seq_len)
        o_hg, sT = _hg_call(zhg, lb_l, gon_l, batch, seq_len, MOBA_BLOCK)
        xp = _post_call(xp, oT.T, o_hg, (mp[2], mp[4], mp[3], mp[5]), gains, w, seq_len, tm)
        k_p.append(k)
        v_p.append(v)
        s_p.append(jnp.swapaxes(sT, -1, -2))

        ms = mod[l, :, batch:n_c]
        z = _dec_in_call(xs, row(g_pre_mix), ms[1], ms[0], w["w_in"])
        a = ATT_WIDTH
        part = lambda i: z[:, i * a:(i + 1) * a].reshape(n_seq, 1, a)
        heads = lambda i: z[:, i * a:(i + 1) * a].reshape(n_seq, ATT_HEADS, ATT_HEAD_DIM)
        o_att = _dec_att_call(page_table, heads(0), heads(1), heads(2), dec_bias, rpe_heads, cache_k,
                              cache_v, l, pages_per_step)
        o_hgs, s_new = _dec_hg_call(part(3), part(4), part(5), part(6), lb_l, gon_l, state_hgrn[l])
        ms4 = ms.reshape(6, 1, n_seq, D_MODEL)
        xs = _post_call(xs, o_att.reshape(n_seq, a), o_hgs.reshape(n_seq, a),
                        (ms4[2], ms4[4], ms4[3], ms4[5]), gains, w, n_seq, n_seq)
        k_s.append(z[:, a:2 * a])
        v_s.append(z[:, 2 * a:3 * a])
        s_s.append(s_new)

    hd = (ATT_HEADS, ATT_HEAD_DIM)
    return (xp.reshape(batch, seq_len, D_MODEL),
            xs.reshape(n_seq, 1, D_MODEL),
            jnp.stack(k_p).reshape(DEPTH, batch, seq_len, *hd),
            jnp.stack(v_p).reshape(DEPTH, batch, seq_len, *hd),
            jnp.stack(s_p),
            jnp.stack(k_s).reshape(DEPTH, n_seq, 1, *hd),
            jnp.stack(v_s).reshape(DEPTH, n_seq, 1, *hd),
            jnp.stack(s_s))
```

```python
import functools
import math

import numpy as np
import jax
import jax.numpy as jnp
from jax import lax
from jax.experimental import pallas as pl
from jax.experimental.pallas import tpu as pltpu

F32 = jnp.float32
BF16 = jnp.bfloat16

D_MODEL = 1024
DEPTH = 4
ATT_HEADS = 8
ATT_HEAD_DIM = 64
ATT_WIDTH = ATT_HEADS * ATT_HEAD_DIM
MOBA_BLOCK = 256
MOBA_TOPK = 3
PAGE_SIZE = 128
RPE_BUCKETS = 32
RPE_MAX_DIST = 128
HG_HEADS = 4
HG_DIM = 128
HG_WIDTH = HG_HEADS * HG_DIM
HG_SUB = 16
D_FF = 4 * D_MODEL
EPS = 1e-6

LANES = 128
HEAD_PAD = 128
GATE_SLOTS = 32
ATT_HEADS_PER_STEP = 4
LOG2E = 1.4426950408889634
Q_SCALE = ATT_HEAD_DIM ** -0.5 * LOG2E
NEG = -30000.0
VMEM_LIMIT = 56 * 1024 * 1024

NT = (((1,), (1,)), ((), ()))


def _bucket_starts():
    max_exact = RPE_BUCKETS // 2
    d = np.arange(0, RPE_MAX_DIST + 1)
    dd = np.maximum(d, max_exact).astype(np.float32)
    large = max_exact + (np.log(dd / np.float32(max_exact)) / np.float32(math.log(RPE_MAX_DIST / max_exact))
                         * np.float32(RPE_BUCKETS - max_exact)).astype(np.int32)
    large = np.minimum(large, RPE_BUCKETS - 1)
    b = np.where(d < max_exact, d, large)
    return tuple(int(np.argmax(b >= k)) for k in range(RPE_BUCKETS))


BUCKET_STARTS = _bucket_starts()


def _params(sem):
    return pltpu.CompilerParams(dimension_semantics=sem, vmem_limit_bytes=VMEM_LIMIT)


def _rms(x, g):
    return x * lax.rsqrt(jnp.mean(x * x, axis=-1, keepdims=True) + EPS) * g


def _dot(a, b):
    return jnp.dot(a, b, preferred_element_type=F32)


def _dot_exact_lhs(a, x):
    hi = x.astype(BF16)
    r = x - hi.astype(F32)
    mid = r.astype(BF16)
    lo = (r - mid.astype(F32)).astype(BF16)
    return _dot(a, hi) + _dot(a, mid) + _dot(a, lo)


def _top_blocks(gate, n, sentinel, hit_ok):
    sel = jnp.zeros(gate.shape, jnp.bool_)
    for j in range(MOBA_TOPK):
        mx = jnp.max(gate, axis=0, keepdims=True)
        idx = jnp.min(jnp.where(gate == mx, n, sentinel), axis=0, keepdims=True)
        hit = n == idx
        sel = sel | (hit & hit_ok(j))
        gate = jnp.where(hit, -jnp.inf, gate)
    return sel


def _ada_kernel(c_ref, w_ref, b_ref, o_ref):
    c = c_ref[...]
    act = (c / (1.0 + jnp.exp(-c))).astype(BF16)
    o_ref[0, 0] = _dot(act, w_ref[0].astype(BF16)) + b_ref[0, 0]


def _ada_call(c_all, w_ada, b_ada):
    rows = c_all.shape[0]
    return pl.pallas_call(
        _ada_kernel,
        grid=(DEPTH, 6),
        in_specs=[pl.BlockSpec((rows, D_MODEL), lambda l, j: (0, 0)),
                  pl.BlockSpec((1, D_MODEL, D_MODEL), lambda l, j: (l, 0, j)),
                  pl.BlockSpec((1, 1, 1, D_MODEL), lambda l, j: (l, j, 0, 0))],
        out_specs=pl.BlockSpec((1, 1, rows, D_MODEL), lambda l, j: (l, j, 0, 0)),
        out_shape=jax.ShapeDtypeStruct((DEPTH, 6, rows, D_MODEL), F32),
        compiler_params=_params(("arbitrary", "arbitrary")),
        name="ada_mod",
    )(c_all, w_ada, b_ada.reshape(DEPTH, 6, 1, D_MODEL))


def _lb_kernel(p_ref, o_ref):
    p = p_ref[...]
    e = jnp.exp(p - jnp.max(p, axis=0, keepdims=True))
    sm = e / jnp.sum(e, axis=0, keepdims=True)
    acc = jnp.zeros((1, HG_WIDTH), F32)
    for l in range(DEPTH):
        o_ref[l:l + 1, :] = acc
        if l + 1 < DEPTH:
            acc = acc + sm[l + 1:l + 2, :]


def _lb_call(lb_param):
    return pl.pallas_call(
        _lb_kernel,
        out_shape=jax.ShapeDtypeStruct((DEPTH, HG_WIDTH), F32),
        name="hgrn_lower_bounds",
    )(lb_param)


def _bias_lookup(d, table):
    val = table(RPE_BUCKETS - 1)
    for b in range(RPE_BUCKETS - 2, -1, -1):
        val = jnp.where(d < BUCKET_STARTS[b + 1], table(b), val)
    return val * LOG2E


def _bias_tile_kernel(rpe_ref, o_ref):
    h = pl.program_id(0)
    kk = lax.broadcasted_iota(jnp.int32, (MOBA_BLOCK, MOBA_BLOCK), 0)
    qq = lax.broadcasted_iota(jnp.int32, (MOBA_BLOCK, MOBA_BLOCK), 1)
    d = qq - kk
    table = lambda b: rpe_ref[h * RPE_BUCKETS + b]
    o_ref[0, 0] = jnp.where(d >= 0, _bias_lookup(d, table), NEG)
    o_ref[0, 1] = _bias_lookup(d + MOBA_BLOCK, table)


def _bias_tile_call(rpe_flat):
    return pl.pallas_call(
        _bias_tile_kernel,
        grid=(ATT_HEADS,),
        in_specs=[pl.BlockSpec(memory_space=pltpu.SMEM)],
        out_specs=pl.BlockSpec((1, 2, MOBA_BLOCK, MOBA_BLOCK), lambda h: (h, 0, 0, 0)),
        out_shape=jax.ShapeDtypeStruct((ATT_HEADS, 2, MOBA_BLOCK, MOBA_BLOCK), F32),
        compiler_params=_params(("arbitrary",)),
        name="rpe_bias_tiles",
    )(rpe_flat)


def _dec_bias_kernel(rpe_ref, o_ref):
    assert BUCKET_STARTS[-1] <= PAGE_SIZE + 1
    d = PAGE_SIZE - lax.broadcasted_iota(jnp.int32, (ATT_HEADS, 1, PAGE_SIZE), 2)
    o_ref[...] = _bias_lookup(d, lambda b: rpe_ref[:, :, b:b + 1])


def _dec_bias_call(rpe_heads):
    return pl.pallas_call(
        _dec_bias_kernel,
        out_shape=jax.ShapeDtypeStruct((ATT_HEADS, 1, PAGE_SIZE), F32),
        name="rpe_bias_decode",
    )(rpe_heads)


def _in_proj_kernel(x_ref, g_ref, sc_ref, sh_ref, wqT_ref, wkT_ref, wvT_ref, wkp_ref, whg_ref,
                    qT_ref, kT_ref, vT_ref, vTb_ref, kaug_ref, kmean_ref, zhg_ref, *, blocks_per_seq):
    x = x_ref[...]
    tm = x.shape[0]
    h = (_rms(x, g_ref[...]) * (1.0 + sc_ref[0]) + sh_ref[0]).astype(BF16)
    transposed = lambda w_ref: lax.dot_general(w_ref[...], h, NT, preferred_element_type=F32)
    qT_ref[0] = (transposed(wqT_ref) * Q_SCALE).astype(BF16)
    kT_ref[0] = transposed(wkT_ref)
    vT = transposed(wvT_ref)
    vT_ref[0] = vT
    vTb_ref[0] = vT.astype(BF16)
    zhg_ref[...] = _dot(h, whg_ref[...])
    kp = _dot(h, wkp_ref[...])
    nb = tm // MOBA_BLOCK
    for r in range(nb):
        kmean_ref[r] = jnp.mean(kp[r * MOBA_BLOCK:(r + 1) * MOBA_BLOCK], axis=0, keepdims=True)
    row = lax.broadcasted_iota(jnp.int32, kp.shape, 0)
    lane = lax.broadcasted_iota(jnp.int32, kp.shape, 1) % HEAD_PAD
    blk = (pl.program_id(0) * nb + row // MOBA_BLOCK) % blocks_per_seq
    onehot = (lane == blk + ATT_HEAD_DIM) | (lane == blk + ATT_HEAD_DIM + GATE_SLOTS)
    kaug_ref[...] = jnp.where(onehot, 1.0, kp).astype(BF16)


def _in_proj_call(x2, g, sc, sh, w, batch, seq_len, tm):
    t = x2.shape[0]
    tps = seq_len // tm
    nbt = tm // MOBA_BLOCK
    hp = ATT_HEADS * HEAD_PAD
    row = lambda i: (i, 0)
    fixed = lambda i: (0, 0)
    mod = lambda i: (i // tps, 0, 0)
    tr = pl.BlockSpec((1, ATT_WIDTH, tm), lambda i: (i // tps, 0, i % tps))
    wspec = lambda a: pl.BlockSpec(a.shape, fixed)
    tshape = lambda dt: jax.ShapeDtypeStruct((batch, ATT_WIDTH, seq_len), dt)
    return pl.pallas_call(
        functools.partial(_in_proj_kernel, blocks_per_seq=seq_len // MOBA_BLOCK),
        grid=(t // tm,),
        in_specs=[pl.BlockSpec((tm, D_MODEL), row),
                  pl.BlockSpec((1, D_MODEL), fixed),
                  pl.BlockSpec((1, 1, D_MODEL), mod),
                  pl.BlockSpec((1, 1, D_MODEL), mod),
                  wspec(w["wqT"]), wspec(w["wkT"]), wspec(w["wvT"]), wspec(w["wkp"]), wspec(w["whg"])],
        out_specs=[tr, tr, tr, tr,
                   pl.BlockSpec((tm, hp), row),
                   pl.BlockSpec((nbt, 1, hp), lambda i: (i, 0, 0)),
                   pl.BlockSpec((tm, 4 * HG_WIDTH), row)],
        out_shape=[tshape(BF16), tshape(F32), tshape(F32), tshape(BF16),
                   jax.ShapeDtypeStruct((t, hp), BF16),
                   jax.ShapeDtypeStruct((t // MOBA_BLOCK, 1, hp), F32),
                   jax.ShapeDtypeStruct((t, 4 * HG_WIDTH), F32)],
        compiler_params=_params(("arbitrary",)),
        name="in_proj",
    )(x2, g, sc, sh, w["wqT"], w["wkT"], w["wvT"], w["wkp"], w["whg"])


def _gate_kernel(rpe_ref, kmean_ref, qT_ref, o_ref, *, nbs):
    own = pl.program_id(1)
    tq = qT_ref.shape[2]
    n = lax.broadcasted_iota(jnp.int32, (nbs, tq), 0)
    far = n <= own - 2
    for h in range(ATT_HEADS):
        q = qT_ref[0, h * ATT_HEAD_DIM:(h + 1) * ATT_HEAD_DIM, :]
        km = kmean_ref[0][:, h * HEAD_PAD:h * HEAD_PAD + ATT_HEAD_DIM].astype(BF16)
        gate = jnp.where(n < own, _dot(km, q), -jnp.inf)
        sel = _top_blocks(gate, n, nbs, lambda j: j < own)
        c = jnp.full((nbs, tq), rpe_ref[h * RPE_BUCKETS + RPE_BUCKETS - 1] * LOG2E, F32)
        c_hi = c.astype(BF16).astype(F32)
        c_lo = c - c_hi
        p_hi = jnp.where(sel, jnp.where(far, c_hi, 0.0), NEG)
        p_hi = jnp.where(n >= own, 0.0, p_hi)
        p_lo = jnp.where(sel & far, c_lo, 0.0)
        base = h * HEAD_PAD
        o_ref[0, base:base + ATT_HEAD_DIM, :] = q
        o_ref[0, base + ATT_HEAD_DIM:base + HEAD_PAD, :] = jnp.zeros((HEAD_PAD - ATT_HEAD_DIM, tq), BF16)
        o_ref[0, base + ATT_HEAD_DIM:base + ATT_HEAD_DIM + nbs, :] = p_hi.astype(BF16)
        o_ref[0, base + ATT_HEAD_DIM + GATE_SLOTS:base + ATT_HEAD_DIM + GATE_SLOTS + nbs, :] = p_lo.astype(BF16)


def _gate_call(rpe_flat, kmean, qT, batch, seq_len):
    nbs = seq_len // MOBA_BLOCK
    hp = ATT_HEADS * HEAD_PAD
    return pl.pallas_call(
        functools.partial(_gate_kernel, nbs=nbs),
        grid=(batch, nbs),
        in_specs=[pl.BlockSpec(memory_space=pltpu.SMEM),
                  pl.BlockSpec((1, nbs, hp), lambda b, i: (b, 0, 0)),
                  pl.BlockSpec((1, ATT_WIDTH, MOBA_BLOCK), lambda b, i: (b, 0, i))],
        out_specs=pl.BlockSpec((1, hp, MOBA_BLOCK), lambda b, i: (b, 0, i)),
        out_shape=jax.ShapeDtypeStruct((batch, hp, seq_len), BF16),
        compiler_params=_params(("arbitrary", "arbitrary")),
        name="moba_gate",
    )(rpe_flat, kmean, qT)


def _att_kernel(q_ref, k_ref, vT_ref, bias_ref, o_ref, s_scr, p_scr, m_scr, l_scr, a_scr, acc_scr, *, heads):
    qi = pl.program_id(2)

    def block(n, which):
        start = pl.multiple_of(n * MOBA_BLOCK, MOBA_BLOCK)
        for h in range(heads):
            q = q_ref[0, h * HEAD_PAD:(h + 1) * HEAD_PAD, :]
            s = _dot(k_ref[pl.ds(start, MOBA_BLOCK), h * HEAD_PAD:(h + 1) * HEAD_PAD], q)
            if which is not None:
                s = s + bias_ref[h, which]
            s_scr[h] = s
        for h in range(heads):
            s = s_scr[h]
            m = m_scr[h]
            m_new = jnp.maximum(m, jnp.max(s, axis=0, keepdims=True))
            alpha = jnp.exp2(m - m_new)
            p = jnp.exp2(s - m_new)
            l_scr[h] = alpha * l_scr[h] + jnp.sum(p, axis=0, keepdims=True)
            m_scr[h] = m_new
            a_scr[h] = alpha
            p_scr[h] = p.astype(BF16)
        for h in range(heads):
            v = vT_ref[0, h * ATT_HEAD_DIM:(h + 1) * ATT_HEAD_DIM, pl.ds(start, MOBA_BLOCK)]
            acc_scr[h] = a_scr[h] * acc_scr[h] + _dot(v, p_scr[h])

    m_scr[...] = jnp.full(m_scr.shape, -jnp.inf, F32)
    l_scr[...] = jnp.zeros(l_scr.shape, F32)
    acc_scr[...] = jnp.zeros(acc_scr.shape, F32)
    block(qi, 0)

    @pl.when(qi >= 1)
    def _():
        block(qi - 1, 1)

    @pl.loop(0, jnp.maximum(qi - 1, 0))
    def _(n):
        block(n, None)

    for h in range(heads):
        o_ref[0, h * ATT_HEAD_DIM:(h + 1) * ATT_HEAD_DIM, :] = (acc_scr[h] / l_scr[h]).astype(BF16)


def _att_call(qaug, kaug, vT, bias_tiles, batch, seq_len):
    nq = seq_len // MOBA_BLOCK
    hb = ATT_HEADS_PER_STEP
    return pl.pallas_call(
        functools.partial(_att_kernel, heads=hb),
        grid=(batch, ATT_HEADS // hb, nq),
        in_specs=[pl.BlockSpec((1, hb * HEAD_PAD, MOBA_BLOCK), lambda b, h, i: (b, h, i)),
                  pl.BlockSpec((seq_len, hb * HEAD_PAD), lambda b, h, i: (b, h)),
                  pl.BlockSpec((1, hb * ATT_HEAD_DIM, seq_len), lambda b, h, i: (b, h, 0)),
                  pl.BlockSpec((hb, 2, MOBA_BLOCK, MOBA_BLOCK), lambda b, h, i: (h, 0, 0, 0))],
        out_specs=pl.BlockSpec((1, hb * ATT_HEAD_DIM, MOBA_BLOCK), lambda b, h, i: (b, h, i)),
        out_shape=jax.ShapeDtypeStruct((batch, ATT_WIDTH, seq_len), BF16),
        scratch_shapes=[pltpu.VMEM((hb, MOBA_BLOCK, MOBA_BLOCK), F32),
                        pltpu.VMEM((hb, MOBA_BLOCK, MOBA_BLOCK), BF16),
                        pltpu.VMEM((hb, 1, MOBA_BLOCK), F32),
                        pltpu.VMEM((hb, 1, MOBA_BLOCK), F32),
                        pltpu.VMEM((hb, 1, MOBA_BLOCK), F32),
                        pltpu.VMEM((hb, ATT_HEAD_DIM, MOBA_BLOCK), F32)],
        compiler_params=_params(("arbitrary", "arbitrary", "arbitrary")),
        name="moba_attention",
    )(qaug, kaug, vT, bias_tiles)


def _log_forget(fpre, lb):
    log_sig = jnp.minimum(fpre, 0.0) - jnp.log(1.0 + jnp.exp(-jnp.abs(fpre)))
    a1 = jnp.log(lb)
    a2 = jnp.log1p(-lb) + log_sig
    return jnp.maximum(a1, a2) + jnp.log(1.0 + jnp.exp(-jnp.abs(a1 - a2)))


def _hg_kernel(qh_ref, fh_ref, ih_ref, gh_ref, lb_ref, gon_ref, o_ref, sT_ref, b_scr, k_scr, v_scr):
    @pl.when(pl.program_id(2) == 0)
    def _():
        sT_ref[...] = jnp.zeros_like(sT_ref)

    ts = qh_ref.shape[0]
    n_sub = ts // HG_SUB
    lb = lb_ref[...]
    fpre = fh_ref[...]
    qh = qh_ref[...]
    v = ih_ref[...]
    logf = _log_forget(fpre, lb)
    kk = (1.0 - lb) / (1.0 + jnp.exp(fpre))
    q = qh / (1.0 + jnp.exp(-qh))

    r = lax.broadcasted_iota(jnp.int32, (ts, ts), 0)
    c = lax.broadcasted_iota(jnp.int32, (ts, ts), 1)
    same = (r // HG_SUB) == (c // HG_SUB)
    tri = jnp.where(same & (c <= r), 1.0, 0.0).astype(BF16)
    ones = jnp.where(same, 1.0, 0.0).astype(BF16)
    b = _dot_exact_lhs(tri, logf)
    b_end = _dot_exact_lhs(ones, logf)
    q_dec = (q * jnp.exp(b)).astype(BF16)
    k_dec = (kk * jnp.exp(b_end - b)).astype(BF16)
    decay = jnp.exp(b_end)

    for scr, val in ((b_scr, b), (k_scr, kk), (v_scr, v)):
        scr[0:HG_SUB, :] = jnp.zeros((HG_SUB, HG_DIM), F32)
        scr[HG_SUB:, :] = val
    tmod = lax.broadcasted_iota(jnp.int32, (ts, HG_DIM), 0) % HG_SUB
    o = jnp.zeros((ts, HG_DIM), F32)
    for dlt in range(HG_SUB):
        rows = slice(HG_SUB - dlt, HG_SUB - dlt + ts)
        bs, ks, vs = b_scr[rows, :], k_scr[rows, :], v_scr[rows, :]
        e = jnp.exp(jnp.where(tmod >= dlt, b - bs, -1e30))
        a = jnp.sum(q * ks * e, axis=-1, keepdims=True)
        o = o + a * vs

    vT = v.T
    lane_sub = lax.broadcasted_iota(jnp.int32, (HG_DIM, ts), 1) // HG_SUB
    sT = sT_ref[0, 0]
    inter = []
    for j in range(n_sub):
        rows = slice(j * HG_SUB, (j + 1) * HG_SUB)
        inter.append(lax.dot_general(q_dec[rows], sT.astype(BF16), NT, preferred_element_type=F32))
        vT_j = jnp.where(lane_sub == j, vT, 0.0).astype(BF16)
        sT = sT * decay[j * HG_SUB:j * HG_SUB + 1, :] + _dot(vT_j, k_dec)
    sT_ref[0, 0] = sT
    o = o + jnp.concatenate(inter, axis=0)

    gh = gh_ref[...]
    o = _rms(o, gon_ref[...]) * (gh / (1.0 + jnp.exp(-gh)))
    o_ref[...] = o.astype(BF16)


def _hg_call(zhg, lb_l, gon_l, batch, seq_len, ts):
    t = zhg.shape[0]
    nt = seq_len // ts
    part = lambda p: pl.BlockSpec((ts, HG_DIM), lambda b, h, i: (b * nt + i, p * HG_HEADS + h))
    vec = pl.BlockSpec((1, HG_DIM), lambda b, h, i: (0, h))
    return pl.pallas_call(
        _hg_kernel,
        grid=(batch, HG_HEADS, nt),
        in_specs=[part(0), part(1), part(2), part(3), vec, vec],
        out_specs=[pl.BlockSpec((ts, HG_DIM), lambda b, h, i: (b * nt + i, h)),
                   pl.BlockSpec((1, 1, HG_DIM, HG_DIM), lambda b, h, i: (b, h, 0, 0))],
        out_shape=[jax.ShapeDtypeStruct((t, HG_WIDTH), BF16),
                   jax.ShapeDtypeStruct((batch, HG_HEADS, HG_DIM, HG_DIM), F32)],
        scratch_shapes=[pltpu.VMEM((HG_SUB + ts, HG_DIM), F32)] * 3,
        compiler_params=_params(("arbitrary", "arbitrary", "arbitrary")),
        name="hgrn2_prompt",
    )(zhg, zhg, zhg, zhg, lb_l, gon_l)


def _post_kernel(x_ref, oa_ref, oh_ref, ga1_ref, sc2_ref, sh2_ref, ga2_ref, gpm_ref, gpf_ref, gqf_ref,
                 woa_ref, woh_ref, wup_ref, wdn_ref, out_ref, *, ff_chunk):
    x = x_ref[...]
    y = _dot(oa_ref[...], woa_ref[...]) + _dot(oh_ref[...], woh_ref[...])
    x1 = x + ga1_ref[0] * _rms(y, gpm_ref[...])
    h2 = (_rms(x1, gpf_ref[...]) * (1.0 + sc2_ref[0]) + sh2_ref[0]).astype(BF16)
    m = jnp.zeros(x.shape, F32)
    for c in range(D_FF // ff_chunk):
        cols = slice(c * ff_chunk, (c + 1) * ff_chunk)
        u = jnp.square(jnp.maximum(_dot(h2, wup_ref[:, cols]), 0.0)).astype(BF16)
        m = m + _dot(u, wdn_ref[cols, :])
    out_ref[...] = x1 + ga2_ref[0] * _rms(m, gqf_ref[...])


def _post_call(x2, oa, oh, mods, gains, w, rows_per_mod, tm):
    t = x2.shape[0]
    row = lambda i: (i, 0)
    fixed = lambda i: (0, 0)
    mod_rows = mods[0].shape[1]
    mod = pl.BlockSpec((1, mod_rows, D_MODEL), lambda i: ((i * tm) // rows_per_mod, 0, 0))
    gain = pl.BlockSpec((1, D_MODEL), fixed)
    wspec = lambda a: pl.BlockSpec(a.shape, fixed)
    return pl.pallas_call(
        functools.partial(_post_kernel, ff_chunk=1024),
        grid=(t // tm,),
        in_specs=[pl.BlockSpec((tm, D_MODEL), row),
                  pl.BlockSpec((tm, ATT_WIDTH), row),
                  pl.BlockSpec((tm, HG_WIDTH), row),
                  mod, mod, mod, mod, gain, gain, gain,
                  wspec(w["woa"]), wspec(w["woh"]), wspec(w["wup"]), wspec(w["wdn"])],
        out_specs=pl.BlockSpec((tm, D_MODEL), row),
        out_shape=jax.ShapeDtypeStruct((t, D_MODEL), F32),
        compiler_params=_params(("arbitrary",)),
        name="out_proj_mlp",
    )(x2, oa, oh, *mods, *gains, w["woa"], w["woh"], w["wup"], w["wdn"])


def _dec_in_kernel(x_ref, g_ref, sc_ref, sh_ref, w_ref, z_ref):
    h = (_rms(x_ref[...], g_ref[...]) * (1.0 + sc_ref[...]) + sh_ref[...]).astype(BF16)
    z_ref[...] = _dot(h, w_ref[...])


def _dec_in_call(x2, g, sc, sh, w_in_l):
    rows = x2.shape[0]
    n_in = w_in_l.shape[1]
    tn = 512
    full = pl.BlockSpec((rows, D_MODEL), lambda j: (0, 0))
    return pl.pallas_call(
        _dec_in_kernel,
        grid=(n_in // tn,),
        in_specs=[full, pl.BlockSpec((1, D_MODEL), lambda j: (0, 0)), full, full,
                  pl.BlockSpec((D_MODEL, tn), lambda j: (0, j))],
        out_specs=pl.BlockSpec((rows, tn), lambda j: (0, j)),
        out_shape=jax.ShapeDtypeStruct((rows, n_in), F32),
        compiler_params=_params(("arbitrary",)),
        name="in_proj_sample",
    )(x2, g, sc, sh, w_in_l)


def _dec_att_kernel(pt_ref, q_ref, kn_ref, vn_ref, bias_ref, rpe_ref, *refs, pages_per_step, n_blocks):
    k_refs = refs[:pages_per_step]
    v_refs = refs[pages_per_step:2 * pages_per_step]
    o_ref = refs[2 * pages_per_step]
    m_s, l_s, g_s, acc_s = refs[2 * pages_per_step + 1:]
    j = pl.program_id(1)
    last_step = j == pl.num_programs(1) - 1
    bps = pages_per_step // 2
    q = q_ref[0] * Q_SCALE
    c_far = rpe_ref[:, :, RPE_BUCKETS - 1:RPE_BUCKETS] * LOG2E
    stat = (1, ATT_HEADS, 1, LANES)

    scores = lambda k_page: jnp.sum(k_page * q, axis=1, keepdims=True)
    lane_sum = lambda x: jnp.sum(x, axis=-1, keepdims=True)
    lane_max = lambda x: jnp.max(x, axis=-1, keepdims=True)

    for blk in range(bps):
        s0 = scores(k_refs[2 * blk][0, 0])
        s1 = scores(k_refs[2 * blk + 1][0, 0])
        g = lane_sum(s0 + s1)
        s0 = s0 + c_far
        if blk == bps - 1:
            s1 = s1 + jnp.where(last_step, bias_ref[...], c_far)
        else:
            s1 = s1 + c_far
        m = jnp.maximum(lane_max(s0), lane_max(s1))
        p0 = jnp.exp2(s0 - m)
        p1 = jnp.exp2(s1 - m)
        row = j * bps + blk
        m_s[pl.ds(row, 1)] = jnp.broadcast_to(m, stat)
        l_s[pl.ds(row, 1)] = jnp.broadcast_to(lane_sum(p0 + p1), stat)
        g_s[pl.ds(row, 1)] = jnp.broadcast_to(g, stat)
        acc_s[pl.ds(row, 1)] = (p0 * v_refs[2 * blk][0, 0] + p1 * v_refs[2 * blk + 1][0, 0])[None]

    @pl.when(last_step)
    def _():
        s_new = scores(kn_ref[0]) + rpe_ref[:, :, 0:1] * LOG2E
        lane = lax.broadcasted_iota(jnp.int32, (ATT_HEADS, ATT_HEAD_DIM, LANES), 2)
        m_s[n_blocks:n_blocks + 1] = s_new[None]
        l_s[n_blocks:n_blocks + 1] = jnp.ones(stat, F32)
        g_s[n_blocks:n_blocks + 1] = jnp.zeros(stat, F32)
        acc_s[n_blocks:n_blocks + 1] = jnp.where(lane == 0, vn_ref[0], 0.0)[None]
        rows = n_blocks + 1
        n = lax.broadcasted_iota(jnp.int32, m_s.shape, 0)
        gate = jnp.where(n < n_blocks, g_s[...], -jnp.inf)
        sel = _top_blocks(gate, n, rows, lambda jj: jj < n_blocks) | (n == n_blocks)
        mm = jnp.where(sel, m_s[...], -jnp.inf)
        w = jnp.exp2(mm - jnp.max(mm, axis=0, keepdims=True))
        den = jnp.sum(w * l_s[...], axis=0)
        num = lane_sum(jnp.sum(w * acc_s[...], axis=0))
        o_ref[0] = jnp.broadcast_to(num / den[:, :, 0:1], o_ref.shape[1:]).astype(BF16)


def _dec_att_call(page_table, q4, kn4, vn4, dec_bias, rpe_heads, cache_kT, cache_vT, layer, pages_per_step):
    n_seq, n_pages = page_table.shape
    n_blocks = n_pages * PAGE_SIZE // MOBA_BLOCK
    steps = n_pages // pages_per_step
    rows = n_blocks + 1
    tile = (ATT_HEADS, ATT_HEAD_DIM, LANES)
    vec = pl.BlockSpec((1, *tile), lambda b, j, pt: (b, 0, 0, 0))

    def page_spec(p):
        return pl.BlockSpec((1, 1, ATT_HEADS, ATT_HEAD_DIM, PAGE_SIZE),
                            lambda b, j, pt: (layer, pt[b, j * pages_per_step + p], 0, 0, 0))

    grid_spec = pltpu.PrefetchScalarGridSpec(
        num_scalar_prefetch=1,
        grid=(n_seq, steps),
        in_specs=[vec, vec, vec,
                  pl.BlockSpec((ATT_HEADS, 1, PAGE_SIZE), lambda b, j, pt: (0, 0, 0)),
                  pl.BlockSpec((ATT_HEADS, 1, RPE_BUCKETS), lambda b, j, pt: (0, 0, 0))]
                 + [page_spec(p) for p in range(pages_per_step)]
                 + [page_spec(p) for p in range(pages_per_step)],
        out_specs=vec,
        scratch_shapes=[pltpu.VMEM((rows, ATT_HEADS, 1, LANES), F32),
                        pltpu.VMEM((rows, ATT_HEADS, 1, LANES), F32),
                        pltpu.VMEM((rows, ATT_HEADS, 1, LANES), F32),
                        pltpu.VMEM((rows, *tile), F32)],
    )
    return pl.pallas_call(
        functools.partial(_dec_att_kernel, pages_per_step=pages_per_step, n_blocks=n_blocks),
        grid_spec=grid_spec,
        out_shape=jax.ShapeDtypeStruct((n_seq, *tile), BF16),
        compiler_params=_params(("arbitrary", "arbitrary")),
        name="moba_attention_sample",
    )(page_table, q4, kn4, vn4, dec_bias, rpe_heads,
      *([cache_kT] * pages_per_step), *([cache_vT] * pages_per_step))


def _dec_hg_kernel(qh_ref, fh_ref, ih_ref, gh_ref, lb_ref, gon_ref, s_ref, o_ref, so_ref):
    r = lax.broadcasted_iota(jnp.int32, (HG_DIM, HG_DIM), 0)
    c = lax.broadcasted_iota(jnp.int32, (HG_DIM, HG_DIM), 1)
    eye = r == c

    def column(x):
        return jnp.sum(jnp.where(eye, jnp.broadcast_to(x, (HG_DIM, HG_DIM)), 0.0), axis=1, keepdims=True)

    for h in range(HG_HEADS):
        cols = slice(h * HG_DIM, (h + 1) * HG_DIM)
        lb = lb_ref[:, cols]
        fpre = fh_ref[0][:, cols]
        qh = qh_ref[0][:, cols]
        gh = gh_ref[0][:, cols]
        v = ih_ref[0][:, cols]
        f = jnp.exp(_log_forget(fpre, lb))
        kk = (1.0 - lb) / (1.0 + jnp.exp(fpre))
        q = qh / (1.0 + jnp.exp(-qh))
        s_new = column(f) * s_ref[0, h] + column(kk) * v
        so_ref[0, h] = s_new
        o = jnp.sum(column(q) * s_new, axis=0, keepdims=True)
        o = _rms(o, gon_ref[:, cols]) * (gh / (1.0 + jnp.exp(-gh)))
        o_ref[0, :, cols] = o.astype(BF16)


def _dec_hg_call(qh3, fh3, ih3, gh3, lb_l, gon_l, state_l):
    n_seq = qh3.shape[0]
    vec = pl.BlockSpec((1, 1, HG_WIDTH), lambda b: (b, 0, 0))
    par = pl.BlockSpec((1, HG_WIDTH), lambda b: (0, 0))
    st = pl.BlockSpec((1, HG_HEADS, HG_DIM, HG_DIM), lambda b: (b, 0, 0, 0))
    return pl.pallas_call(
        _dec_hg_kernel,
        grid=(n_seq,),
        in_specs=[vec, vec, vec, vec, par, par, st],
        out_specs=[vec, st],
        out_shape=[jax.ShapeDtypeStruct((n_seq, 1, HG_WIDTH), BF16),
                   jax.ShapeDtypeStruct(state_l.shape, F32)],
        compiler_params=_params(("arbitrary",)),
        name="hgrn2_sample",
    )(qh3, fh3, ih3, gh3, lb_l, gon_l, state_l)


def _layer_weights(w_in, w_out, w_up, w_down, l):
    a = ATT_WIDTH
    wi = w_in[l]
    wq, wk, wv, whg = wi[:, :a], wi[:, a:2 * a], wi[:, 2 * a:3 * a], wi[:, 3 * a:]
    wkp = jnp.pad(wk.reshape(D_MODEL, ATT_HEADS, ATT_HEAD_DIM),
                  ((0, 0), (0, 0), (0, HEAD_PAD - ATT_HEAD_DIM))).reshape(D_MODEL, ATT_HEADS * HEAD_PAD)
    return {"wqT": wq.T, "wkT": wk.T, "wvT": wv.T, "wkp": wkp, "whg": whg, "w_in": wi,
            "woa": w_out[l, :a], "woh": w_out[l, a:], "wup": w_up[l], "wdn": w_down[l]}


def kernel(x_prompt, x_sample, c_prompt, c_sample, cache_k, cache_v, state_hgrn, page_table, w_ada, b_ada,
           g_pre_mix, g_post_mix, g_pre_ffn, g_post_ffn, w_in, lb_param, g_onorm, w_out, w_up, w_down,
           rpe_table):
    batch, seq_len, _ = x_prompt.shape
    n_seq, dec_seq, _ = x_sample.shape
    n_pages = page_table.shape[1]
    assert dec_seq == 1 and seq_len % MOBA_BLOCK == 0 and (n_pages * PAGE_SIZE) % MOBA_BLOCK == 0
    assert seq_len // MOBA_BLOCK <= GATE_SLOTS
    tm = 512 if seq_len % 512 == 0 else MOBA_BLOCK
    pages_per_step = 8 if n_pages % 8 == 0 else 2
    a = ATT_WIDTH
    hd = (ATT_HEADS, ATT_HEAD_DIM)

    w_in_b, w_out_b = w_in.astype(BF16), w_out.astype(BF16)
    w_up_b, w_down_b = w_up.astype(BF16), w_down.astype(BF16)

    n_c = batch + n_seq
    c_rows = ((n_c + 7) // 8) * 8
    c_all = jnp.pad(jnp.concatenate([c_prompt, c_sample], axis=0), ((0, c_rows - n_c), (0, 0)))
    mod = _ada_call(c_all, w_ada, b_ada)
    lb = _lb_call(lb_param)
    rpe_flat = rpe_table.T.reshape(-1)
    rpe_heads = rpe_table.T.reshape(ATT_HEADS, 1, RPE_BUCKETS)
    bias_tiles = _bias_tile_call(rpe_flat)
    dec_bias = _dec_bias_call(rpe_heads)
    cache_kT = jnp.transpose(cache_k, (0, 1, 3, 4, 2))
    cache_vT = jnp.transpose(cache_v, (0, 1, 3, 4, 2))

    xp = x_prompt.reshape(batch * seq_len, D_MODEL)
    xs = x_sample.reshape(n_seq, D_MODEL)
    k_p, v_p, s_p, k_s, v_s, s_s = [], [], [], [], [], []
    for l in range(DEPTH):
        w = _layer_weights(w_in_b, w_out_b, w_up_b, w_down_b, l)
        row = lambda arr: arr[l].reshape(1, -1)
        gains = (row(g_post_mix), row(g_pre_ffn), row(g_post_ffn))
        lb_l, gon_l = row(lb), row(g_onorm)

        mp = mod[l, :, :batch].reshape(6, batch, 1, D_MODEL)
        qT, kT, vT, vTb, kaug, kmean, zhg = _in_proj_call(xp, row(g_pre_mix), mp[1], mp[0], w, batch,
                                                          seq_len, tm)
        qaug = _gate_call(rpe_flat, kmean.reshape(batch, seq_len // MOBA_BLOCK, -1), qT, batch, seq_len)
        oT = _att_call(qaug, kaug, vTb, bias_tiles, batch, seq_len)
        o_att = jnp.swapaxes(oT, 1, 2).reshape(batch * seq_len, a)
        o_hg, sT = _hg_call(zhg, lb_l, gon_l, batch, seq_len, MOBA_BLOCK)
        xp = _post_call(xp, o_att, o_hg, (mp[2], mp[4], mp[3], mp[5]), gains, w, seq_len, tm)
        k_p.append(kT)
        v_p.append(vT)
        s_p.append(jnp.swapaxes(sT, -1, -2))

        ms = mod[l, :, batch:n_c]
        z = _dec_in_call(xs, row(g_pre_mix), ms[1], ms[0], w["w_in"])
        part = lambda i: z[:, i * a:(i + 1) * a]
        wide = lambda i: jnp.broadcast_to(part(i).reshape(n_seq, *hd, 1), (n_seq, *hd, LANES))
        o4 = _dec_att_call(page_table, wide(0), wide(1), wide(2), dec_bias, rpe_heads, cache_kT, cache_vT,
                           l, pages_per_step)
        o_att_s = o4[..., 0].reshape(n_seq, a)
        row3 = lambda i: part(i).reshape(n_seq, 1, a)
        o_hgs, s_new = _dec_hg_call(row3(3), row3(4), row3(5), row3(6), lb_l, gon_l, state_hgrn[l])
        ms4 = ms.reshape(6, 1, n_seq, D_MODEL)
        xs = _post_call(xs, o_att_s, o_hgs.reshape(n_seq, a), (ms4[2], ms4[4], ms4[3], ms4[5]), gains, w,
                        n_seq, n_seq)
        k_s.append(part(1))
        v_s.append(part(2))
        s_s.append(s_new)

    rows_out = lambda parts: jnp.transpose(jnp.stack(parts).reshape(DEPTH, batch, *hd, seq_len),
                                           (0, 1, 4, 2, 3))
    return (xp.reshape(batch, seq_len, D_MODEL),
            xs.reshape(n_seq, 1, D_MODEL),
            rows_out(k_p),
            rows_out(v_p),
            jnp.stack(s_p),
            jnp.stack(k_s).reshape(DEPTH, n_seq, 1, *hd),
            jnp.stack(v_s).reshape(DEPTH, n_seq, 1, *hd),
            jnp.stack(s_s))
```

```python
import functools
import math

import numpy as np
import jax
import jax.numpy as jnp
from jax import lax
from jax.experimental import pallas as pl
from jax.experimental.pallas import tpu as pltpu

F32 = jnp.float32
BF16 = jnp.bfloat16

D_MODEL = 1024
DEPTH = 4
ATT_HEADS = 8
ATT_HEAD_DIM = 64
ATT_WIDTH = ATT_HEADS * ATT_HEAD_DIM
MOBA_BLOCK = 256
MOBA_TOPK = 3
PAGE_SIZE = 128
RPE_BUCKETS = 32
RPE_MAX_DIST = 128
HG_HEADS = 4
HG_DIM = 128
HG_WIDTH = HG_HEADS * HG_DIM
HG_SUB = 16
D_FF = 4 * D_MODEL
EPS = 1e-6

LANES = 128
HEAD_PAD = 128
GATE_SLOTS = 32
ATT_HEADS_PER_STEP = 4
ATT_SPAN = 4
ATT_DEN_ROWS = 16
LOG2E = 1.4426950408889634
Q_SCALE = ATT_HEAD_DIM ** -0.5 * LOG2E
NEG = -30000.0
VMEM_LIMIT = 56 * 1024 * 1024

NT = (((1,), (1,)), ((), ()))


def _bucket_starts():
    max_exact = RPE_BUCKETS // 2
    d = np.arange(0, RPE_MAX_DIST + 1)
    dd = np.maximum(d, max_exact).astype(np.float32)
    large = max_exact + (np.log(dd / np.float32(max_exact)) / np.float32(math.log(RPE_MAX_DIST / max_exact))
                         * np.float32(RPE_BUCKETS - max_exact)).astype(np.int32)
    large = np.minimum(large, RPE_BUCKETS - 1)
    b = np.where(d < max_exact, d, large)
    return tuple(int(np.argmax(b >= k)) for k in range(RPE_BUCKETS))


BUCKET_STARTS = _bucket_starts()


def _params(sem):
    return pltpu.CompilerParams(dimension_semantics=sem, vmem_limit_bytes=VMEM_LIMIT)


def _rms(x, g):
    return x * lax.rsqrt(jnp.mean(x * x, axis=-1, keepdims=True) + EPS) * g


def _dot(a, b):
    return jnp.dot(a, b, preferred_element_type=F32)


def _dot_exact_lhs(a, x):
    hi = x.astype(BF16)
    r = x - hi.astype(F32)
    mid = r.astype(BF16)
    lo = (r - mid.astype(F32)).astype(BF16)
    return _dot(a, hi) + _dot(a, mid) + _dot(a, lo)


def _top_blocks(gate, n, sentinel, hit_ok):
    sel = jnp.zeros(gate.shape, jnp.bool_)
    for j in range(MOBA_TOPK):
        mx = jnp.max(gate, axis=0, keepdims=True)
        idx = jnp.min(jnp.where(gate == mx, n, sentinel), axis=0, keepdims=True)
        hit = n == idx
        sel = sel | (hit & hit_ok(j))
        gate = jnp.where(hit, -jnp.inf, gate)
    return sel


def _ada_kernel(c_ref, w_ref, b_ref, o_ref):
    c = c_ref[...]
    act = (c / (1.0 + jnp.exp(-c))).astype(BF16)
    o_ref[0, 0] = _dot(act, w_ref[0].astype(BF16)) + b_ref[0, 0]


def _ada_call(c_all, w_ada, b_ada):
    rows = c_all.shape[0]
    return pl.pallas_call(
        _ada_kernel,
        grid=(DEPTH, 6),
        in_specs=[pl.BlockSpec((rows, D_MODEL), lambda l, j: (0, 0)),
                  pl.BlockSpec((1, D_MODEL, D_MODEL), lambda l, j: (l, 0, j)),
                  pl.BlockSpec((1, 1, 1, D_MODEL), lambda l, j: (l, j, 0, 0))],
        out_specs=pl.BlockSpec((1, 1, rows, D_MODEL), lambda l, j: (l, j, 0, 0)),
        out_shape=jax.ShapeDtypeStruct((DEPTH, 6, rows, D_MODEL), F32),
        compiler_params=_params(("arbitrary", "arbitrary")),
        name="ada_mod",
    )(c_all, w_ada, b_ada.reshape(DEPTH, 6, 1, D_MODEL))


def _lb_kernel(p_ref, o_ref):
    p = p_ref[...]
    e = jnp.exp(p - jnp.max(p, axis=0, keepdims=True))
    sm = e / jnp.sum(e, axis=0, keepdims=True)
    acc = jnp.zeros((1, HG_WIDTH), F32)
    for l in range(DEPTH):
        o_ref[l:l + 1, :] = acc
        if l + 1 < DEPTH:
            acc = acc + sm[l + 1:l + 2, :]


def _lb_call(lb_param):
    return pl.pallas_call(
        _lb_kernel,
        out_shape=jax.ShapeDtypeStruct((DEPTH, HG_WIDTH), F32),
        name="hgrn_lower_bounds",
    )(lb_param)


def _bias_lookup(d, table):
    val = table(RPE_BUCKETS - 1)
    for b in range(RPE_BUCKETS - 2, -1, -1):
        val = jnp.where(d < BUCKET_STARTS[b + 1], table(b), val)
    return val * LOG2E


def _bias_tile_kernel(rpe_ref, o_ref):
    h = pl.program_id(0)
    kk = lax.broadcasted_iota(jnp.int32, (MOBA_BLOCK, MOBA_BLOCK), 0)
    qq = lax.broadcasted_iota(jnp.int32, (MOBA_BLOCK, MOBA_BLOCK), 1)
    d = qq - kk
    table = lambda b: rpe_ref[h * RPE_BUCKETS + b]
    o_ref[0, 0] = jnp.where(d >= 0, _bias_lookup(d, table), NEG)
    o_ref[0, 1] = _bias_lookup(d + MOBA_BLOCK, table)


def _bias_tile_call(rpe_flat):
    return pl.pallas_call(
        _bias_tile_kernel,
        grid=(ATT_HEADS,),
        in_specs=[pl.BlockSpec(memory_space=pltpu.SMEM)],
        out_specs=pl.BlockSpec((1, 2, MOBA_BLOCK, MOBA_BLOCK), lambda h: (h, 0, 0, 0)),
        out_shape=jax.ShapeDtypeStruct((ATT_HEADS, 2, MOBA_BLOCK, MOBA_BLOCK), F32),
        compiler_params=_params(("arbitrary",)),
        name="rpe_bias_tiles",
    )(rpe_flat)


def _dec_bias_kernel(rpe_ref, o_ref):
    assert BUCKET_STARTS[-1] <= PAGE_SIZE + 1
    d = PAGE_SIZE - lax.broadcasted_iota(jnp.int32, (ATT_HEADS, 1, PAGE_SIZE), 2)
    o_ref[...] = _bias_lookup(d, lambda b: rpe_ref[:, :, b:b + 1])


def _dec_bias_call(rpe_heads):
    return pl.pallas_call(
        _dec_bias_kernel,
        out_shape=jax.ShapeDtypeStruct((ATT_HEADS, 1, PAGE_SIZE), F32),
        name="rpe_bias_decode",
    )(rpe_heads)


def _in_proj_kernel(x_ref, g_ref, sc_ref, sh_ref, wqT_ref, wkT_ref, wvT_ref, wkp_ref, whg_ref,
                    qT_ref, kT_ref, vT_ref, vTb_ref, kaug_ref, kmean_ref, zhg_ref, *, blocks_per_seq):
    x = x_ref[...]
    tm = x.shape[0]
    h = (_rms(x, g_ref[...]) * (1.0 + sc_ref[0]) + sh_ref[0]).astype(BF16)
    transposed = lambda w_ref: lax.dot_general(w_ref[...], h, NT, preferred_element_type=F32)
    qT_ref[0] = (transposed(wqT_ref) * Q_SCALE).astype(BF16)
    kT_ref[0] = transposed(wkT_ref)
    vT = transposed(wvT_ref)
    vT_ref[0] = vT
    vTb_ref[0] = vT.astype(BF16)
    zhg_ref[...] = _dot(h, whg_ref[...])
    kp = _dot(h, wkp_ref[...])
    nb = tm // MOBA_BLOCK
    for r in range(nb):
        kmean_ref[r] = jnp.mean(kp[r * MOBA_BLOCK:(r + 1) * MOBA_BLOCK], axis=0, keepdims=True)
    row = lax.broadcasted_iota(jnp.int32, kp.shape, 0)
    lane = lax.broadcasted_iota(jnp.int32, kp.shape, 1) % HEAD_PAD
    blk = (pl.program_id(0) * nb + row // MOBA_BLOCK) % blocks_per_seq
    onehot = (lane == blk + ATT_HEAD_DIM) | (lane == blk + ATT_HEAD_DIM + GATE_SLOTS)
    kaug_ref[...] = jnp.where(onehot, 1.0, kp).astype(BF16)


def _in_proj_call(x2, g, sc, sh, w, batch, seq_len, tm):
    t = x2.shape[0]
    tps = seq_len // tm
    nbt = tm // MOBA_BLOCK
    hp = ATT_HEADS * HEAD_PAD
    row = lambda i: (i, 0)
    fixed = lambda i: (0, 0)
    mod = lambda i: (i // tps, 0, 0)
    tr = pl.BlockSpec((1, ATT_WIDTH, tm), lambda i: (i // tps, 0, i % tps))
    wspec = lambda a: pl.BlockSpec(a.shape, fixed)
    tshape = lambda dt: jax.ShapeDtypeStruct((batch, ATT_WIDTH, seq_len), dt)
    return pl.pallas_call(
        functools.partial(_in_proj_kernel, blocks_per_seq=seq_len // MOBA_BLOCK),
        grid=(t // tm,),
        in_specs=[pl.BlockSpec((tm, D_MODEL), row),
                  pl.BlockSpec((1, D_MODEL), fixed),
                  pl.BlockSpec((1, 1, D_MODEL), mod),
                  pl.BlockSpec((1, 1, D_MODEL), mod),
                  wspec(w["wqT"]), wspec(w["wkT"]), wspec(w["wvT"]), wspec(w["wkp"]), wspec(w["whg"])],
        out_specs=[tr, tr, tr, tr,
                   pl.BlockSpec((tm, hp), row),
                   pl.BlockSpec((nbt, 1, hp), lambda i: (i, 0, 0)),
                   pl.BlockSpec((tm, 4 * HG_WIDTH), row)],
        out_shape=[tshape(BF16), tshape(F32), tshape(F32), tshape(BF16),
                   jax.ShapeDtypeStruct((t, hp), BF16),
                   jax.ShapeDtypeStruct((t // MOBA_BLOCK, 1, hp), F32),
                   jax.ShapeDtypeStruct((t, 4 * HG_WIDTH), F32)],
        compiler_params=_params(("arbitrary",)),
        name="in_proj",
    )(x2, g, sc, sh, w["wqT"], w["wkT"], w["wvT"], w["wkp"], w["whg"])


def _gate_kernel(rpe_ref, kmean_ref, qT_ref, o_ref, *, nbs):
    own = pl.program_id(1)
    tq = qT_ref.shape[2]
    n = lax.broadcasted_iota(jnp.int32, (nbs, tq), 0)
    far = n <= own - 2
    for h in range(ATT_HEADS):
        q = qT_ref[0, h * ATT_HEAD_DIM:(h + 1) * ATT_HEAD_DIM, :]
        km = kmean_ref[0][:, h * HEAD_PAD:h * HEAD_PAD + ATT_HEAD_DIM].astype(BF16)
        gate = jnp.where(n < own, _dot(km, q), -jnp.inf)
        sel = _top_blocks(gate, n, nbs, lambda j: j < own)
        c = jnp.full((nbs, tq), rpe_ref[h * RPE_BUCKETS + RPE_BUCKETS - 1] * LOG2E, F32)
        c_hi = c.astype(BF16).astype(F32)
        c_lo = c - c_hi
        p_hi = jnp.where(sel, jnp.where(far, c_hi, 0.0), NEG)
        p_hi = jnp.where(n >= own, 0.0, p_hi)
        p_lo = jnp.where(sel & far, c_lo, 0.0)
        base = h * HEAD_PAD
        o_ref[0, base:base + ATT_HEAD_DIM, :] = q
        o_ref[0, base + ATT_HEAD_DIM:base + HEAD_PAD, :] = jnp.zeros((HEAD_PAD - ATT_HEAD_DIM, tq), BF16)
        o_ref[0, base + ATT_HEAD_DIM:base + ATT_HEAD_DIM + nbs, :] = p_hi.astype(BF16)
        o_ref[0, base + ATT_HEAD_DIM + GATE_SLOTS:base + ATT_HEAD_DIM + GATE_SLOTS + nbs, :] = p_lo.astype(BF16)


def _gate_call(rpe_flat, kmean, qT, batch, seq_len):
    nbs = seq_len // MOBA_BLOCK
    hp = ATT_HEADS * HEAD_PAD
    return pl.pallas_call(
        functools.partial(_gate_kernel, nbs=nbs),
        grid=(batch, nbs),
        in_specs=[pl.BlockSpec(memory_space=pltpu.SMEM),
                  pl.BlockSpec((1, nbs, hp), lambda b, i: (b, 0, 0)),
                  pl.BlockSpec((1, ATT_WIDTH, MOBA_BLOCK), lambda b, i: (b, 0, i))],
        out_specs=pl.BlockSpec((1, hp, MOBA_BLOCK), lambda b, i: (b, 0, i)),
        out_shape=jax.ShapeDtypeStruct((batch, hp, seq_len), BF16),
        compiler_params=_params(("arbitrary", "arbitrary")),
        name="moba_gate",
    )(rpe_flat, kmean, qT)


def _att_kernel(q_ref, k_ref, vT_ref, bias_ref, o_ref, s_a, s_b, x_a, x_b, p_scr, m_scr, a_scr, acc_scr, *,
                heads, n_key_blocks):
    qi = pl.program_id(2)
    ones = jnp.ones((ATT_DEN_ROWS, ATT_SPAN * MOBA_BLOCK), BF16)

    def scores(first_block, n_blocks, which, s_scr, x_scr):
        keys = n_blocks * MOBA_BLOCK
        start = pl.multiple_of(first_block * MOBA_BLOCK, MOBA_BLOCK)
        for h in range(heads):
            q = q_ref[0, h * HEAD_PAD:(h + 1) * HEAD_PAD, :]
            s = _dot(k_ref[pl.ds(start, keys), h * HEAD_PAD:(h + 1) * HEAD_PAD], q)
            if which is not None:
                s = s + bias_ref[h, which]
            s_scr[h, 0:keys, :] = s
            x_scr[h] = jnp.max(s, axis=0, keepdims=True)

    def absorb(first_block, n_blocks, s_scr, x_scr):
        keys = n_blocks * MOBA_BLOCK
        start = pl.multiple_of(first_block * MOBA_BLOCK, MOBA_BLOCK)
        for h in range(heads):
            m = m_scr[h]
            m_new = jnp.maximum(m, x_scr[h])
            a_scr[h] = jnp.exp2(m - m_new)
            m_scr[h] = m_new
            p_scr[h, 0:keys, :] = jnp.exp2(s_scr[h, 0:keys, :] - m_new).astype(BF16)
        for h in range(heads):
            v = vT_ref[0, h * ATT_HEAD_DIM:(h + 1) * ATT_HEAD_DIM, pl.ds(start, keys)]
            v1 = jnp.concatenate([v, ones[:, 0:keys]], axis=0)
            acc_scr[h] = a_scr[h] * acc_scr[h] + _dot(v1, p_scr[h, 0:keys, :])

    def span(first_block, n_blocks, which):
        scores(first_block, n_blocks, which, s_a, x_a)
        absorb(first_block, n_blocks, s_a, x_a)

    m_scr[...] = jnp.full(m_scr.shape, -jnp.inf, F32)
    acc_scr[...] = jnp.zeros(acc_scr.shape, F32)
    span(qi, 1, 0)

    @pl.when(qi >= 1)
    def _():
        span(qi - 1, 1, 1)

    n_far = jnp.maximum(qi - 1, 0)
    n_full = n_far // ATT_SPAN
    ahead = lambda i: jnp.minimum(i * ATT_SPAN, n_key_blocks - ATT_SPAN)

    @pl.when(n_full > 0)
    def _():
        scores(0, ATT_SPAN, None, s_a, x_a)

    @pl.loop(0, n_full // 2)
    def _(j):
        scores(ahead(2 * j + 1), ATT_SPAN, None, s_b, x_b)
        absorb(2 * j * ATT_SPAN, ATT_SPAN, s_a, x_a)
        scores(ahead(2 * j + 2), ATT_SPAN, None, s_a, x_a)
        absorb((2 * j + 1) * ATT_SPAN, ATT_SPAN, s_b, x_b)

    @pl.when(n_full % 2 == 1)
    def _():
        absorb((n_full - 1) * ATT_SPAN, ATT_SPAN, s_a, x_a)

    done = n_full * ATT_SPAN
    size = ATT_SPAN // 2
    while size >= 1:
        @pl.when((n_far - done) & size != 0)
        def _(done=done, size=size):
            span(done, size, None)
        done = done + ((n_far - done) & size)
        size //= 2

    for h in range(heads):
        acc = acc_scr[h]
        den = acc[ATT_HEAD_DIM:ATT_HEAD_DIM + 1, :]
        o_ref[0, h * ATT_HEAD_DIM:(h + 1) * ATT_HEAD_DIM, :] = (acc[0:ATT_HEAD_DIM, :] / den).astype(BF16)


def _att_call(qaug, kaug, vT, bias_tiles, batch, seq_len):
    nq = seq_len // MOBA_BLOCK
    hb = ATT_HEADS_PER_STEP
    return pl.pallas_call(
        functools.partial(_att_kernel, heads=hb, n_key_blocks=nq),
        grid=(batch, ATT_HEADS // hb, nq),
        in_specs=[pl.BlockSpec((1, hb * HEAD_PAD, MOBA_BLOCK), lambda b, h, i: (b, h, i)),
                  pl.BlockSpec((seq_len, hb * HEAD_PAD), lambda b, h, i: (b, h)),
                  pl.BlockSpec((1, hb * ATT_HEAD_DIM, seq_len), lambda b, h, i: (b, h, 0)),
                  pl.BlockSpec((hb, 2, MOBA_BLOCK, MOBA_BLOCK), lambda b, h, i: (h, 0, 0, 0))],
        out_specs=pl.BlockSpec((1, hb * ATT_HEAD_DIM, MOBA_BLOCK), lambda b, h, i: (b, h, i)),
        out_shape=jax.ShapeDtypeStruct((batch, ATT_WIDTH, seq_len), BF16),
        scratch_shapes=[pltpu.VMEM((hb, ATT_SPAN * MOBA_BLOCK, MOBA_BLOCK), F32),
                        pltpu.VMEM((hb, ATT_SPAN * MOBA_BLOCK, MOBA_BLOCK), F32),
                        pltpu.VMEM((hb, 1, MOBA_BLOCK), F32),
                        pltpu.VMEM((hb, 1, MOBA_BLOCK), F32),
                        pltpu.VMEM((hb, ATT_SPAN * MOBA_BLOCK, MOBA_BLOCK), BF16),
                        pltpu.VMEM((hb, 1, MOBA_BLOCK), F32),
                        pltpu.VMEM((hb, 1, MOBA_BLOCK), F32),
                        pltpu.VMEM((hb, ATT_HEAD_DIM + ATT_DEN_ROWS, MOBA_BLOCK), F32)],
        compiler_params=_params(("arbitrary", "arbitrary", "arbitrary")),
        name="moba_attention",
    )(qaug, kaug, vT, bias_tiles)


def _log_forget(fpre, lb):
    log_sig = jnp.minimum(fpre, 0.0) - jnp.log(1.0 + jnp.exp(-jnp.abs(fpre)))
    a1 = jnp.log(lb)
    a2 = jnp.log1p(-lb) + log_sig
    return jnp.maximum(a1, a2) + jnp.log(1.0 + jnp.exp(-jnp.abs(a1 - a2)))


def _hg_kernel(qh_ref, fh_ref, ih_ref, gh_ref, lb_ref, gon_ref, o_ref, sT_ref, b_scr, k_scr, v_scr):
    @pl.when(pl.program_id(2) == 0)
    def _():
        sT_ref[...] = jnp.zeros_like(sT_ref)

    ts = qh_ref.shape[0]
    n_sub = ts // HG_SUB
    lb = lb_ref[...]
    fpre = fh_ref[...]
    qh = qh_ref[...]
    v = ih_ref[...]
    logf = _log_forget(fpre, lb)
    kk = (1.0 - lb) / (1.0 + jnp.exp(fpre))
    q = qh / (1.0 + jnp.exp(-qh))

    r = lax.broadcasted_iota(jnp.int32, (ts, ts), 0)
    c = lax.broadcasted_iota(jnp.int32, (ts, ts), 1)
    same = (r // HG_SUB) == (c // HG_SUB)
    tri = jnp.where(same & (c <= r), 1.0, 0.0).astype(BF16)
    ones = jnp.where(same, 1.0, 0.0).astype(BF16)
    b = _dot_exact_lhs(tri, logf)
    b_end = _dot_exact_lhs(ones, logf)
    q_dec = (q * jnp.exp(b)).astype(BF16)
    k_dec = (kk * jnp.exp(b_end - b)).astype(BF16)
    decay = jnp.exp(b_end)

    for scr, val in ((b_scr, b), (k_scr, kk), (v_scr, v)):
        scr[0:HG_SUB, :] = jnp.zeros((HG_SUB, HG_DIM), F32)
        scr[HG_SUB:, :] = val
    tmod = lax.broadcasted_iota(jnp.int32, (ts, HG_DIM), 0) % HG_SUB
    o = jnp.zeros((ts, HG_DIM), F32)
    for dlt in range(HG_SUB):
        rows = slice(HG_SUB - dlt, HG_SUB - dlt + ts)
        bs, ks, vs = b_scr[rows, :], k_scr[rows, :], v_scr[rows, :]
        e = jnp.exp(jnp.where(tmod >= dlt, b - bs, -1e30))
        a = jnp.sum(q * ks * e, axis=-1, keepdims=True)
        o = o + a * vs

    vT = v.T
    lane_sub = lax.broadcasted_iota(jnp.int32, (HG_DIM, ts), 1) // HG_SUB
    sT = sT_ref[0, 0]
    inter = []
    for j in range(n_sub):
        rows = slice(j * HG_SUB, (j + 1) * HG_SUB)
        inter.append(lax.dot_general(q_dec[rows], sT.astype(BF16), NT, preferred_element_type=F32))
        vT_j = jnp.where(lane_sub == j, vT, 0.0).astype(BF16)
        sT = sT * decay[j * HG_SUB:j * HG_SUB + 1, :] + _dot(vT_j, k_dec)
    sT_ref[0, 0] = sT
    o = o + jnp.concatenate(inter, axis=0)

    gh = gh_ref[...]
    o = _rms(o, gon_ref[...]) * (gh / (1.0 + jnp.exp(-gh)))
    o_ref[...] = o.astype(BF16)


def _hg_call(zhg, lb_l, gon_l, batch, seq_len, ts):
    t = zhg.shape[0]
    nt = seq_len // ts
    part = lambda p: pl.BlockSpec((ts, HG_DIM), lambda b, h, i: (b * nt + i, p * HG_HEADS + h))
    vec = pl.BlockSpec((1, HG_DIM), lambda b, h, i: (0, h))
    return pl.pallas_call(
        _hg_kernel,
        grid=(batch, HG_HEADS, nt),
        in_specs=[part(0), part(1), part(2), part(3), vec, vec],
        out_specs=[pl.BlockSpec((ts, HG_DIM), lambda b, h, i: (b * nt + i, h)),
                   pl.BlockSpec((1, 1, HG_DIM, HG_DIM), lambda b, h, i: (b, h, 0, 0))],
        out_shape=[jax.ShapeDtypeStruct((t, HG_WIDTH), BF16),
                   jax.ShapeDtypeStruct((batch, HG_HEADS, HG_DIM, HG_DIM), F32)],
        scratch_shapes=[pltpu.VMEM((HG_SUB + ts, HG_DIM), F32)] * 3,
        compiler_params=_params(("arbitrary", "arbitrary", "arbitrary")),
        name="hgrn2_prompt",
    )(zhg, zhg, zhg, zhg, lb_l, gon_l)


def _post_kernel(x_ref, oa_ref, oh_ref, ga1_ref, sc2_ref, sh2_ref, ga2_ref, gpm_ref, gpf_ref, gqf_ref,
                 woa_ref, woh_ref, wup_ref, wdn_ref, out_ref, *, ff_chunk):
    x = x_ref[...]
    y = _dot(oa_ref[...], woa_ref[...]) + _dot(oh_ref[...], woh_ref[...])
    x1 = x + ga1_ref[0] * _rms(y, gpm_ref[...])
    h2 = (_rms(x1, gpf_ref[...]) * (1.0 + sc2_ref[0]) + sh2_ref[0]).astype(BF16)
    m = jnp.zeros(x.shape, F32)
    for c in range(D_FF // ff_chunk):
        cols = slice(c * ff_chunk, (c + 1) * ff_chunk)
        u = jnp.square(jnp.maximum(_dot(h2, wup_ref[:, cols]), 0.0)).astype(BF16)
        m = m + _dot(u, wdn_ref[cols, :])
    out_ref[...] = x1 + ga2_ref[0] * _rms(m, gqf_ref[...])


def _post_call(x2, oa, oh, mods, gains, w, rows_per_mod, tm):
    t = x2.shape[0]
    row = lambda i: (i, 0)
    fixed = lambda i: (0, 0)
    mod_rows = mods[0].shape[1]
    mod = pl.BlockSpec((1, mod_rows, D_MODEL), lambda i: ((i * tm) // rows_per_mod, 0, 0))
    gain = pl.BlockSpec((1, D_MODEL), fixed)
    wspec = lambda a: pl.BlockSpec(a.shape, fixed)
    return pl.pallas_call(
        functools.partial(_post_kernel, ff_chunk=1024),
        grid=(t // tm,),
        in_specs=[pl.BlockSpec((tm, D_MODEL), row),
                  pl.BlockSpec((tm, ATT_WIDTH), row),
                  pl.BlockSpec((tm, HG_WIDTH), row),
                  mod, mod, mod, mod, gain, gain, gain,
                  wspec(w["woa"]), wspec(w["woh"]), wspec(w["wup"]), wspec(w["wdn"])],
        out_specs=pl.BlockSpec((tm, D_MODEL), row),
        out_shape=jax.ShapeDtypeStruct((t, D_MODEL), F32),
        compiler_params=_params(("arbitrary",)),
        name="out_proj_mlp",
    )(x2, oa, oh, *mods, *gains, w["woa"], w["woh"], w["wup"], w["wdn"])


def _dec_in_kernel(x_ref, g_ref, sc_ref, sh_ref, w_ref, z_ref):
    h = (_rms(x_ref[...], g_ref[...]) * (1.0 + sc_ref[...]) + sh_ref[...]).astype(BF16)
    z_ref[...] = _dot(h, w_ref[...])


def _dec_in_call(x2, g, sc, sh, w_in_l):
    rows = x2.shape[0]
    n_in = w_in_l.shape[1]
    tn = 512
    full = pl.BlockSpec((rows, D_MODEL), lambda j: (0, 0))
    return pl.pallas_call(
        _dec_in_kernel,
        grid=(n_in // tn,),
        in_specs=[full, pl.BlockSpec((1, D_MODEL), lambda j: (0, 0)), full, full,
                  pl.BlockSpec((D_MODEL, tn), lambda j: (0, j))],
        out_specs=pl.BlockSpec((rows, tn), lambda j: (0, j)),
        out_shape=jax.ShapeDtypeStruct((rows, n_in), F32),
        compiler_params=_params(("arbitrary",)),
        name="in_proj_sample",
    )(x2, g, sc, sh, w_in_l)


def _dec_att_kernel(pt_ref, q_ref, kn_ref, vn_ref, bias_ref, rpe_ref, *refs, pages_per_step, n_blocks):
    k_refs = refs[:pages_per_step]
    v_refs = refs[pages_per_step:2 * pages_per_step]
    o_ref = refs[2 * pages_per_step]
    m_s, l_s, g_s, acc_s = refs[2 * pages_per_step + 1:]
    j = pl.program_id(1)
    last_step = j == pl.num_programs(1) - 1
    bps = pages_per_step // 2
    q = q_ref[0] * Q_SCALE
    c_far = rpe_ref[:, :, RPE_BUCKETS - 1:RPE_BUCKETS] * LOG2E
    stat = (1, ATT_HEADS, 1, LANES)

    scores = lambda k_page: jnp.sum(k_page * q, axis=1, keepdims=True)
    lane_sum = lambda x: jnp.sum(x, axis=-1, keepdims=True)
    lane_max = lambda x: jnp.max(x, axis=-1, keepdims=True)

    for blk in range(bps):
        s0 = scores(k_refs[2 * blk][0, 0])
        s1 = scores(k_refs[2 * blk + 1][0, 0])
        g = lane_sum(s0 + s1)
        s0 = s0 + c_far
        if blk == bps - 1:
            s1 = s1 + jnp.where(last_step, bias_ref[...], c_far)
        else:
            s1 = s1 + c_far
        m = jnp.maximum(lane_max(s0), lane_max(s1))
        p0 = jnp.exp2(s0 - m)
        p1 = jnp.exp2(s1 - m)
        row = j * bps + blk
        m_s[pl.ds(row, 1)] = jnp.broadcast_to(m, stat)
        l_s[pl.ds(row, 1)] = jnp.broadcast_to(lane_sum(p0 + p1), stat)
        g_s[pl.ds(row, 1)] = jnp.broadcast_to(g, stat)
        acc_s[pl.ds(row, 1)] = (p0 * v_refs[2 * blk][0, 0] + p1 * v_refs[2 * blk + 1][0, 0])[None]

    @pl.when(last_step)
    def _():
        s_new = scores(kn_ref[0]) + rpe_ref[:, :, 0:1] * LOG2E
        lane = lax.broadcasted_iota(jnp.int32, (ATT_HEADS, ATT_HEAD_DIM, LANES), 2)
        m_s[n_blocks:n_blocks + 1] = s_new[None]
        l_s[n_blocks:n_blocks + 1] = jnp.ones(stat, F32)
        g_s[n_blocks:n_blocks + 1] = jnp.zeros(stat, F32)
        acc_s[n_blocks:n_blocks + 1] = jnp.where(lane == 0, vn_ref[0], 0.0)[None]
        rows = n_blocks + 1
        n = lax.broadcasted_iota(jnp.int32, m_s.shape, 0)
        gate = jnp.where(n < n_blocks, g_s[...], -jnp.inf)
        sel = _top_blocks(gate, n, rows, lambda jj: jj < n_blocks) | (n == n_blocks)
        mm = jnp.where(sel, m_s[...], -jnp.inf)
        w = jnp.exp2(mm - jnp.max(mm, axis=0, keepdims=True))
        den = jnp.sum(w * l_s[...], axis=0)
        num = lane_sum(jnp.sum(w * acc_s[...], axis=0))
        o_ref[0] = jnp.broadcast_to(num / den[:, :, 0:1], o_ref.shape[1:]).astype(BF16)


def _dec_att_call(page_table, q4, kn4, vn4, dec_bias, rpe_heads, cache_kT, cache_vT, layer, pages_per_step):
    n_seq, n_pages = page_table.shape
    n_blocks = n_pages * PAGE_SIZE // MOBA_BLOCK
    steps = n_pages // pages_per_step
    rows = n_blocks + 1
    tile = (ATT_HEADS, ATT_HEAD_DIM, LANES)
    vec = pl.BlockSpec((1, *tile), lambda b, j, pt: (b, 0, 0, 0))

    def page_spec(p):
        return pl.BlockSpec((1, 1, ATT_HEADS, ATT_HEAD_DIM, PAGE_SIZE),
                            lambda b, j, pt: (layer, pt[b, j * pages_per_step + p], 0, 0, 0))

    grid_spec = pltpu.PrefetchScalarGridSpec(
        num_scalar_prefetch=1,
        grid=(n_seq, steps),
        in_specs=[vec, vec, vec,
                  pl.BlockSpec((ATT_HEADS, 1, PAGE_SIZE), lambda b, j, pt: (0, 0, 0)),
                  pl.BlockSpec((ATT_HEADS, 1, RPE_BUCKETS), lambda b, j, pt: (0, 0, 0))]
                 + [page_spec(p) for p in range(pages_per_step)]
                 + [page_spec(p) for p in range(pages_per_step)],
        out_specs=vec,
        scratch_shapes=[pltpu.VMEM((rows, ATT_HEADS, 1, LANES), F32),
                        pltpu.VMEM((rows, ATT_HEADS, 1, LANES), F32),
                        pltpu.VMEM((rows, ATT_HEADS, 1, LANES), F32),
                        pltpu.VMEM((rows, *tile), F32)],
    )
    return pl.pallas_call(
        functools.partial(_dec_att_kernel, pages_per_step=pages_per_step, n_blocks=n_blocks),
        grid_spec=grid_spec,
        out_shape=jax.ShapeDtypeStruct((n_seq, *tile), BF16),
        compiler_params=_params(("arbitrary", "arbitrary")),
        name="moba_attention_sample",
    )(page_table, q4, kn4, vn4, dec_bias, rpe_heads,
      *([cache_kT] * pages_per_step), *([cache_vT] * pages_per_step))


def _dec_hg_kernel(qh_ref, fh_ref, ih_ref, gh_ref, lb_ref, gon_ref, s_ref, o_ref, so_ref):
    r = lax.broadcasted_iota(jnp.int32, (HG_DIM, HG_DIM), 0)
    c = lax.broadcasted_iota(jnp.int32, (HG_DIM, HG_DIM), 1)
    eye = r == c

    def column(x):
        return jnp.sum(jnp.where(eye, jnp.broadcast_to(x, (HG_DIM, HG_DIM)), 0.0), axis=1, keepdims=True)

    for h in range(HG_HEADS):
        cols = slice(h * HG_DIM, (h + 1) * HG_DIM)
        lb = lb_ref[:, cols]
        fpre = fh_ref[0][:, cols]
        qh = qh_ref[0][:, cols]
        gh = gh_ref[0][:, cols]
        v = ih_ref[0][:, cols]
        f = jnp.exp(_log_forget(fpre, lb))
        kk = (1.0 - lb) / (1.0 + jnp.exp(fpre))
        q = qh / (1.0 + jnp.exp(-qh))
        s_new = column(f) * s_ref[0, h] + column(kk) * v
        so_ref[0, h] = s_new
        o = jnp.sum(column(q) * s_new, axis=0, keepdims=True)
        o = _rms(o, gon_ref[:, cols]) * (gh / (1.0 + jnp.exp(-gh)))
        o_ref[0, :, cols] = o.astype(BF16)


def _dec_hg_call(qh3, fh3, ih3, gh3, lb_l, gon_l, state_l):
    n_seq = qh3.shape[0]
    vec = pl.BlockSpec((1, 1, HG_WIDTH), lambda b: (b, 0, 0))
    par = pl.BlockSpec((1, HG_WIDTH), lambda b: (0, 0))
    st = pl.BlockSpec((1, HG_HEADS, HG_DIM, HG_DIM), lambda b: (b, 0, 0, 0))
    return pl.pallas_call(
        _dec_hg_kernel,
        grid=(n_seq,),
        in_specs=[vec, vec, vec, vec, par, par, st],
        out_specs=[vec, st],
        out_shape=[jax.ShapeDtypeStruct((n_seq, 1, HG_WIDTH), BF16),
                   jax.ShapeDtypeStruct(state_l.shape, F32)],
        compiler_params=_params(("arbitrary",)),
        name="hgrn2_sample",
    )(qh3, fh3, ih3, gh3, lb_l, gon_l, state_l)


def _layer_weights(w_in, w_out, w_up, w_down, l):
    a = ATT_WIDTH
    wi = w_in[l]
    wq, wk, wv, whg = wi[:, :a], wi[:, a:2 * a], wi[:, 2 * a:3 * a], wi[:, 3 * a:]
    wkp = jnp.pad(wk.reshape(D_MODEL, ATT_HEADS, ATT_HEAD_DIM),
                  ((0, 0), (0, 0), (0, HEAD_PAD - ATT_HEAD_DIM))).reshape(D_MODEL, ATT_HEADS * HEAD_PAD)
    return {"wqT": wq.T, "wkT": wk.T, "wvT": wv.T, "wkp": wkp, "whg": whg, "w_in": wi,
            "woa": w_out[l, :a], "woh": w_out[l, a:], "wup": w_up[l], "wdn": w_down[l]}


def kernel(x_prompt, x_sample, c_prompt, c_sample, cache_k, cache_v, state_hgrn, page_table, w_ada, b_ada,
           g_pre_mix, g_post_mix, g_pre_ffn, g_post_ffn, w_in, lb_param, g_onorm, w_out, w_up, w_down,
           rpe_table):
    batch, seq_len, _ = x_prompt.shape
    n_seq, dec_seq, _ = x_sample.shape
    n_pages = page_table.shape[1]
    assert dec_seq == 1 and seq_len % MOBA_BLOCK == 0 and (n_pages * PAGE_SIZE) % MOBA_BLOCK == 0
    assert seq_len // MOBA_BLOCK <= GATE_SLOTS
    tm = 512 if seq_len % 512 == 0 else MOBA_BLOCK
    pages_per_step = next(p for p in (16, 8, 4, 2) if n_pages % p == 0)
    a = ATT_WIDTH
    hd = (ATT_HEADS, ATT_HEAD_DIM)

    w_in_b, w_out_b = w_in.astype(BF16), w_out.astype(BF16)
    w_up_b, w_down_b = w_up.astype(BF16), w_down.astype(BF16)

    n_c = batch + n_seq
    c_rows = ((n_c + 7) // 8) * 8
    c_all = jnp.pad(jnp.concatenate([c_prompt, c_sample], axis=0), ((0, c_rows - n_c), (0, 0)))
    mod = _ada_call(c_all, w_ada, b_ada)
    lb = _lb_call(lb_param)
    rpe_flat = rpe_table.T.reshape(-1)
    rpe_heads = rpe_table.T.reshape(ATT_HEADS, 1, RPE_BUCKETS)
    bias_tiles = _bias_tile_call(rpe_flat)
    dec_bias = _dec_bias_call(rpe_heads)
    cache_kT = jnp.transpose(cache_k, (0, 1, 3, 4, 2))
    cache_vT = jnp.transpose(cache_v, (0, 1, 3, 4, 2))

    xp = x_prompt.reshape(batch * seq_len, D_MODEL)
    xs = x_sample.reshape(n_seq, D_MODEL)
    k_p, v_p, s_p, k_s, v_s, s_s = [], [], [], [], [], []
    for l in range(DEPTH):
        w = _layer_weights(w_in_b, w_out_b, w_up_b, w_down_b, l)
        row = lambda arr: arr[l].reshape(1, -1)
        gains = (row(g_post_mix), row(g_pre_ffn), row(g_post_ffn))
        lb_l, gon_l = row(lb), row(g_onorm)

        mp = mod[l, :, :batch].reshape(6, batch, 1, D_MODEL)
        qT, kT, vT, vTb, kaug, kmean, zhg = _in_proj_call(xp, row(g_pre_mix), mp[1], mp[0], w, batch,
                                                          seq_len, tm)
        qaug = _gate_call(rpe_flat, kmean.reshape(batch, seq_len // MOBA_BLOCK, -1), qT, batch, seq_len)
        oT = _att_call(qaug, kaug, vTb, bias_tiles, batch, seq_len)
        o_att = jnp.swapaxes(oT, 1, 2).reshape(batch * seq_len, a)
        o_hg, sT = _hg_call(zhg, lb_l, gon_l, batch, seq_len, MOBA_BLOCK)
        xp = _post_call(xp, o_att, o_hg, (mp[2], mp[4], mp[3], mp[5]), gains, w, seq_len, tm)
        k_p.append(kT)
        v_p.append(vT)
        s_p.append(jnp.swapaxes(sT, -1, -2))

        ms = mod[l, :, batch:n_c]
        z = _dec_in_call(xs, row(g_pre_mix), ms[1], ms[0], w["w_in"])
        part = lambda i: z[:, i * a:(i + 1) * a]
        wide = lambda i: jnp.broadcast_to(part(i).reshape(n_seq, *hd, 1), (n_seq, *hd, LANES))
        o4 = _dec_att_call(page_table, wide(0), wide(1), wide(2), dec_bias, rpe_heads, cache_kT, cache_vT,
                           l, pages_per_step)
        o_att_s = o4[..., 0].reshape(n_seq, a)
        row3 = lambda i: part(i).reshape(n_seq, 1, a)
        o_hgs, s_new = _dec_hg_call(row3(3), row3(4), row3(5), row3(6), lb_l, gon_l, state_hgrn[l])
        ms4 = ms.reshape(6, 1, n_seq, D_MODEL)
        xs = _post_call(xs, o_att_s, o_hgs.reshape(n_seq, a), (ms4[2], ms4[4], ms4[3], ms4[5]), gains, w,
                        n_seq, n_seq)
        k_s.append(part(1))
        v_s.append(part(2))
        s_s.append(s_new)

    rows_out = lambda parts: jnp.transpose(jnp.stack(parts).reshape(DEPTH, batch, *hd, seq_len),
                                           (0, 1, 4, 2, 3))
    return (xp.reshape(batch, seq_len, D_MODEL),
            xs.reshape(n_seq, 1, D_MODEL),
            rows_out(k_p),
            rows_out(v_p),
            jnp.stack(s_p),
            jnp.stack(k_s).reshape(DEPTH, n_seq, 1, *hd),
            jnp.stack(v_s).reshape(DEPTH, n_seq, 1, *hd),
            jnp.stack(s_s))
```

```python
import functools
import math

import numpy as np
import jax
import jax.numpy as jnp
from jax import lax
from jax.experimental import pallas as pl
from jax.experimental.pallas import tpu as pltpu

F32 = jnp.float32
BF16 = jnp.bfloat16

D_MODEL = 1024
DEPTH = 4
ATT_HEADS = 8
ATT_HEAD_DIM = 64
ATT_WIDTH = ATT_HEADS * ATT_HEAD_DIM
MOBA_BLOCK = 256
MOBA_TOPK = 3
PAGE_SIZE = 128
RPE_BUCKETS = 32
RPE_MAX_DIST = 128
HG_HEADS = 4
HG_DIM = 128
HG_WIDTH = HG_HEADS * HG_DIM
D_FF = 4 * D_MODEL
EPS = 1e-6

LANES = 128
HEAD_PAD = 128
GATE_SLOTS = 32
ATT_HEADS_PER_STEP = 4
ATT_SPAN = 4
ATT_DEN_ROWS = 16
HG_HEADS_PER_STEP = 4
LOG2E = 1.4426950408889634
Q_SCALE = ATT_HEAD_DIM ** -0.5 * LOG2E
NEG = -30000.0
VMEM_LIMIT = 56 * 1024 * 1024

NT = (((1,), (1,)), ((), ()))


def _bucket_starts():
    max_exact = RPE_BUCKETS // 2
    d = np.arange(0, RPE_MAX_DIST + 1)
    dd = np.maximum(d, max_exact).astype(np.float32)
    large = max_exact + (np.log(dd / np.float32(max_exact)) / np.float32(math.log(RPE_MAX_DIST / max_exact))
                         * np.float32(RPE_BUCKETS - max_exact)).astype(np.int32)
    large = np.minimum(large, RPE_BUCKETS - 1)
    b = np.where(d < max_exact, d, large)
    return tuple(int(np.argmax(b >= k)) for k in range(RPE_BUCKETS))


BUCKET_STARTS = _bucket_starts()


def _params(sem):
    return pltpu.CompilerParams(dimension_semantics=sem, vmem_limit_bytes=VMEM_LIMIT)


def _rms(x, g):
    return x * lax.rsqrt(jnp.mean(x * x, axis=-1, keepdims=True) + EPS) * g


def _dot(a, b):
    return jnp.dot(a, b, preferred_element_type=F32)


def _dot_exact_lhs(a, x):
    hi = x.astype(BF16)
    r = x - hi.astype(F32)
    mid = r.astype(BF16)
    lo = (r - mid.astype(F32)).astype(BF16)
    return _dot(a, hi) + _dot(a, mid) + _dot(a, lo)


def _top_blocks(gate, n, sentinel, hit_ok):
    sel = jnp.zeros(gate.shape, jnp.bool_)
    for j in range(MOBA_TOPK):
        mx = jnp.max(gate, axis=0, keepdims=True)
        idx = jnp.min(jnp.where(gate == mx, n, sentinel), axis=0, keepdims=True)
        hit = n == idx
        sel = sel | (hit & hit_ok(j))
        gate = jnp.where(hit, -jnp.inf, gate)
    return sel


def _ada_kernel(c_ref, w_ref, b_ref, o_ref):
    c = c_ref[...]
    act = (c / (1.0 + jnp.exp(-c))).astype(BF16)
    o_ref[0, 0] = _dot(act, w_ref[0].astype(BF16)) + b_ref[0, 0]


def _ada_call(c_all, w_ada, b_ada):
    rows = c_all.shape[0]
    return pl.pallas_call(
        _ada_kernel,
        grid=(DEPTH, 6),
        in_specs=[pl.BlockSpec((rows, D_MODEL), lambda l, j: (0, 0)),
                  pl.BlockSpec((1, D_MODEL, D_MODEL), lambda l, j: (l, 0, j)),
                  pl.BlockSpec((1, 1, 1, D_MODEL), lambda l, j: (l, j, 0, 0))],
        out_specs=pl.BlockSpec((1, 1, rows, D_MODEL), lambda l, j: (l, j, 0, 0)),
        out_shape=jax.ShapeDtypeStruct((DEPTH, 6, rows, D_MODEL), F32),
        compiler_params=_params(("arbitrary", "arbitrary")),
        name="ada_mod",
    )(c_all, w_ada, b_ada.reshape(DEPTH, 6, 1, D_MODEL))


def _lb_kernel(p_ref, o_ref):
    p = p_ref[...]
    e = jnp.exp(p - jnp.max(p, axis=0, keepdims=True))
    sm = e / jnp.sum(e, axis=0, keepdims=True)
    acc = jnp.zeros((1, HG_WIDTH), F32)
    for l in range(DEPTH):
        o_ref[l:l + 1, :] = acc
        if l + 1 < DEPTH:
            acc = acc + sm[l + 1:l + 2, :]


def _lb_call(lb_param):
    return pl.pallas_call(
        _lb_kernel,
        out_shape=jax.ShapeDtypeStruct((DEPTH, HG_WIDTH), F32),
        name="hgrn_lower_bounds",
    )(lb_param)


def _bias_lookup(d, table):
    val = table(RPE_BUCKETS - 1)
    for b in range(RPE_BUCKETS - 2, -1, -1):
        val = jnp.where(d < BUCKET_STARTS[b + 1], table(b), val)
    return val * LOG2E


def _bias_tile_kernel(rpe_ref, o_ref):
    h = pl.program_id(0)
    kk = lax.broadcasted_iota(jnp.int32, (MOBA_BLOCK, MOBA_BLOCK), 0)
    qq = lax.broadcasted_iota(jnp.int32, (MOBA_BLOCK, MOBA_BLOCK), 1)
    d = qq - kk
    table = lambda b: rpe_ref[h * RPE_BUCKETS + b]
    o_ref[0, 0] = jnp.where(d >= 0, _bias_lookup(d, table), NEG)
    o_ref[0, 1] = _bias_lookup(d + MOBA_BLOCK, table)


def _bias_tile_call(rpe_flat):
    return pl.pallas_call(
        _bias_tile_kernel,
        grid=(ATT_HEADS,),
        in_specs=[pl.BlockSpec(memory_space=pltpu.SMEM)],
        out_specs=pl.BlockSpec((1, 2, MOBA_BLOCK, MOBA_BLOCK), lambda h: (h, 0, 0, 0)),
        out_shape=jax.ShapeDtypeStruct((ATT_HEADS, 2, MOBA_BLOCK, MOBA_BLOCK), F32),
        compiler_params=_params(("arbitrary",)),
        name="rpe_bias_tiles",
    )(rpe_flat)


def _dec_bias_kernel(rpe_ref, o_ref):
    assert BUCKET_STARTS[-1] <= PAGE_SIZE + 1
    d = PAGE_SIZE - lax.broadcasted_iota(jnp.int32, (ATT_HEADS, 1, PAGE_SIZE), 2)
    o_ref[...] = _bias_lookup(d, lambda b: rpe_ref[:, :, b:b + 1])


def _dec_bias_call(rpe_heads):
    return pl.pallas_call(
        _dec_bias_kernel,
        out_shape=jax.ShapeDtypeStruct((ATT_HEADS, 1, PAGE_SIZE), F32),
        name="rpe_bias_decode",
    )(rpe_heads)


def _in_proj_kernel(x_ref, g_ref, sc_ref, sh_ref, wqT_ref, wkT_ref, wvT_ref, wkp_ref, whg_ref,
                    qT_ref, kT_ref, vT_ref, vTb_ref, kaug_ref, kmean_ref, zhg_ref, *, blocks_per_seq):
    x = x_ref[...]
    tm = x.shape[0]
    h = (_rms(x, g_ref[...]) * (1.0 + sc_ref[0]) + sh_ref[0]).astype(BF16)
    transposed = lambda w_ref: lax.dot_general(w_ref[...], h, NT, preferred_element_type=F32)
    qT_ref[0] = (transposed(wqT_ref) * Q_SCALE).astype(BF16)
    kT_ref[0] = transposed(wkT_ref)
    vT = transposed(wvT_ref)
    vT_ref[0] = vT
    vTb_ref[0] = vT.astype(BF16)
    zhg_ref[...] = _dot(h, whg_ref[...])
    kp = _dot(h, wkp_ref[...])
    nb = tm // MOBA_BLOCK
    for r in range(nb):
        kmean_ref[r] = jnp.mean(kp[r * MOBA_BLOCK:(r + 1) * MOBA_BLOCK], axis=0, keepdims=True)
    row = lax.broadcasted_iota(jnp.int32, kp.shape, 0)
    lane = lax.broadcasted_iota(jnp.int32, kp.shape, 1) % HEAD_PAD
    blk = (pl.program_id(0) * nb + row // MOBA_BLOCK) % blocks_per_seq
    onehot = (lane == blk + ATT_HEAD_DIM) | (lane == blk + ATT_HEAD_DIM + GATE_SLOTS)
    kaug_ref[...] = jnp.where(onehot, 1.0, kp).astype(BF16)


def _in_proj_call(x2, g, sc, sh, w, batch, seq_len, tm):
    t = x2.shape[0]
    tps = seq_len // tm
    nbt = tm // MOBA_BLOCK
    hp = ATT_HEADS * HEAD_PAD
    row = lambda i: (i, 0)
    fixed = lambda i: (0, 0)
    mod = lambda i: (i // tps, 0, 0)
    tr = pl.BlockSpec((1, ATT_WIDTH, tm), lambda i: (i // tps, 0, i % tps))
    wspec = lambda a: pl.BlockSpec(a.shape, fixed)
    tshape = lambda dt: jax.ShapeDtypeStruct((batch, ATT_WIDTH, seq_len), dt)
    return pl.pallas_call(
        functools.partial(_in_proj_kernel, blocks_per_seq=seq_len // MOBA_BLOCK),
        grid=(t // tm,),
        in_specs=[pl.BlockSpec((tm, D_MODEL), row),
                  pl.BlockSpec((1, D_MODEL), fixed),
                  pl.BlockSpec((1, 1, D_MODEL), mod),
                  pl.BlockSpec((1, 1, D_MODEL), mod),
                  wspec(w["wqT"]), wspec(w["wkT"]), wspec(w["wvT"]), wspec(w["wkp"]), wspec(w["whg"])],
        out_specs=[tr, tr, tr, tr,
                   pl.BlockSpec((tm, hp), row),
                   pl.BlockSpec((nbt, 1, hp), lambda i: (i, 0, 0)),
                   pl.BlockSpec((tm, 4 * HG_WIDTH), row)],
        out_shape=[tshape(BF16), tshape(F32), tshape(F32), tshape(BF16),
                   jax.ShapeDtypeStruct((t, hp), BF16),
                   jax.ShapeDtypeStruct((t // MOBA_BLOCK, 1, hp), F32),
                   jax.ShapeDtypeStruct((t, 4 * HG_WIDTH), F32)],
        compiler_params=_params(("arbitrary",)),
        name="in_proj",
    )(x2, g, sc, sh, w["wqT"], w["wkT"], w["wvT"], w["wkp"], w["whg"])


def _gate_kernel(rpe_ref, kmean_ref, qT_ref, o_ref, *, nbs):
    own = pl.program_id(1)
    tq = qT_ref.shape[2]
    n = lax.broadcasted_iota(jnp.int32, (nbs, tq), 0)
    far = n <= own - 2
    for h in range(ATT_HEADS):
        q = qT_ref[0, h * ATT_HEAD_DIM:(h + 1) * ATT_HEAD_DIM, :]
        km = kmean_ref[0][:, h * HEAD_PAD:h * HEAD_PAD + ATT_HEAD_DIM].astype(BF16)
        gate = jnp.where(n < own, _dot(km, q), -jnp.inf)
        sel = _top_blocks(gate, n, nbs, lambda j: j < own)
        c = jnp.full((nbs, tq), rpe_ref[h * RPE_BUCKETS + RPE_BUCKETS - 1] * LOG2E, F32)
        c_hi = c.astype(BF16).astype(F32)
        c_lo = c - c_hi
        p_hi = jnp.where(sel, jnp.where(far, c_hi, 0.0), NEG)
        p_hi = jnp.where(n >= own, 0.0, p_hi)
        p_lo = jnp.where(sel & far, c_lo, 0.0)
        base = h * HEAD_PAD
        o_ref[0, base:base + ATT_HEAD_DIM, :] = q
        o_ref[0, base + ATT_HEAD_DIM:base + HEAD_PAD, :] = jnp.zeros((HEAD_PAD - ATT_HEAD_DIM, tq), BF16)
        o_ref[0, base + ATT_HEAD_DIM:base + ATT_HEAD_DIM + nbs, :] = p_hi.astype(BF16)
        o_ref[0, base + ATT_HEAD_DIM + GATE_SLOTS:base + ATT_HEAD_DIM + GATE_SLOTS + nbs, :] = p_lo.astype(BF16)


def _gate_call(rpe_flat, kmean, qT, batch, seq_len):
    nbs = seq_len // MOBA_BLOCK
    hp = ATT_HEADS * HEAD_PAD
    return pl.pallas_call(
        functools.partial(_gate_kernel, nbs=nbs),
        grid=(batch, nbs),
        in_specs=[pl.BlockSpec(memory_space=pltpu.SMEM),
                  pl.BlockSpec((1, nbs, hp), lambda b, i: (b, 0, 0)),
                  pl.BlockSpec((1, ATT_WIDTH, MOBA_BLOCK), lambda b, i: (b, 0, i))],
        out_specs=pl.BlockSpec((1, hp, MOBA_BLOCK), lambda b, i: (b, 0, i)),
        out_shape=jax.ShapeDtypeStruct((batch, hp, seq_len), BF16),
        compiler_params=_params(("arbitrary", "arbitrary")),
        name="moba_gate",
    )(rpe_flat, kmean, qT)


def _att_kernel(q_ref, k_ref, vT_ref, bias_ref, o_ref, s_a, s_b, x_a, x_b, p_scr, m_scr, a_scr, acc_scr, *,
                heads, n_key_blocks):
    qi = pl.program_id(2)
    ones = jnp.ones((ATT_DEN_ROWS, ATT_SPAN * MOBA_BLOCK), BF16)

    def scores(first_block, n_blocks, which, s_scr, x_scr):
        keys = n_blocks * MOBA_BLOCK
        start = pl.multiple_of(first_block * MOBA_BLOCK, MOBA_BLOCK)
        for h in range(heads):
            q = q_ref[0, h * HEAD_PAD:(h + 1) * HEAD_PAD, :]
            s = _dot(k_ref[pl.ds(start, keys), h * HEAD_PAD:(h + 1) * HEAD_PAD], q)
            if which is not None:
                s = s + bias_ref[h, which]
            s_scr[h, 0:keys, :] = s
            x_scr[h] = jnp.max(s, axis=0, keepdims=True)

    def absorb(first_block, n_blocks, s_scr, x_scr):
        keys = n_blocks * MOBA_BLOCK
        start = pl.multiple_of(first_block * MOBA_BLOCK, MOBA_BLOCK)
        for h in range(heads):
            m = m_scr[h]
            m_new = jnp.maximum(m, x_scr[h])
            a_scr[h] = jnp.exp2(m - m_new)
            m_scr[h] = m_new
            p_scr[h, 0:keys, :] = jnp.exp2(s_scr[h, 0:keys, :] - m_new).astype(BF16)
        for h in range(heads):
            v = vT_ref[0, h * ATT_HEAD_DIM:(h + 1) * ATT_HEAD_DIM, pl.ds(start, keys)]
            v1 = jnp.concatenate([v, ones[:, 0:keys]], axis=0)
            acc_scr[h] = a_scr[h] * acc_scr[h] + _dot(v1, p_scr[h, 0:keys, :])

    def span(first_block, n_blocks, which):
        scores(first_block, n_blocks, which, s_a, x_a)
        absorb(first_block, n_blocks, s_a, x_a)

    m_scr[...] = jnp.full(m_scr.shape, -jnp.inf, F32)
    acc_scr[...] = jnp.zeros(acc_scr.shape, F32)
    span(qi, 1, 0)

    @pl.when(qi >= 1)
    def _():
        span(qi - 1, 1, 1)

    n_far = jnp.maximum(qi - 1, 0)
    n_full = n_far // ATT_SPAN
    ahead = lambda i: jnp.minimum(i * ATT_SPAN, n_key_blocks - ATT_SPAN)

    @pl.when(n_full > 0)
    def _():
        scores(0, ATT_SPAN, None, s_a, x_a)

    @pl.loop(0, n_full // 2)
    def _(j):
        scores(ahead(2 * j + 1), ATT_SPAN, None, s_b, x_b)
        absorb(2 * j * ATT_SPAN, ATT_SPAN, s_a, x_a)
        scores(ahead(2 * j + 2), ATT_SPAN, None, s_a, x_a)
        absorb((2 * j + 1) * ATT_SPAN, ATT_SPAN, s_b, x_b)

    @pl.when(n_full % 2 == 1)
    def _():
        absorb((n_full - 1) * ATT_SPAN, ATT_SPAN, s_a, x_a)

    done = n_full * ATT_SPAN
    size = ATT_SPAN // 2
    while size >= 1:
        @pl.when((n_far - done) & size != 0)
        def _(done=done, size=size):
            span(done, size, None)
        done = done + ((n_far - done) & size)
        size //= 2

    for h in range(heads):
        acc = acc_scr[h]
        den = acc[ATT_HEAD_DIM:ATT_HEAD_DIM + 1, :]
        o_ref[0, h * ATT_HEAD_DIM:(h + 1) * ATT_HEAD_DIM, :] = (acc[0:ATT_HEAD_DIM, :] / den).astype(BF16)


def _att_call(qaug, kaug, vT, bias_tiles, batch, seq_len):
    nq = seq_len // MOBA_BLOCK
    hb = ATT_HEADS_PER_STEP
    return pl.pallas_call(
        functools.partial(_att_kernel, heads=hb, n_key_blocks=nq),
        grid=(batch, ATT_HEADS // hb, nq),
        in_specs=[pl.BlockSpec((1, hb * HEAD_PAD, MOBA_BLOCK), lambda b, h, i: (b, h, i)),
                  pl.BlockSpec((seq_len, hb * HEAD_PAD), lambda b, h, i: (b, h)),
                  pl.BlockSpec((1, hb * ATT_HEAD_DIM, seq_len), lambda b, h, i: (b, h, 0)),
                  pl.BlockSpec((hb, 2, MOBA_BLOCK, MOBA_BLOCK), lambda b, h, i: (h, 0, 0, 0))],
        out_specs=pl.BlockSpec((1, hb * ATT_HEAD_DIM, MOBA_BLOCK), lambda b, h, i: (b, h, i)),
        out_shape=jax.ShapeDtypeStruct((batch, ATT_WIDTH, seq_len), BF16),
        scratch_shapes=[pltpu.VMEM((hb, ATT_SPAN * MOBA_BLOCK, MOBA_BLOCK), F32),
                        pltpu.VMEM((hb, ATT_SPAN * MOBA_BLOCK, MOBA_BLOCK), F32),
                        pltpu.VMEM((hb, 1, MOBA_BLOCK), F32),
                        pltpu.VMEM((hb, 1, MOBA_BLOCK), F32),
                        pltpu.VMEM((hb, ATT_SPAN * MOBA_BLOCK, MOBA_BLOCK), BF16),
                        pltpu.VMEM((hb, 1, MOBA_BLOCK), F32),
                        pltpu.VMEM((hb, 1, MOBA_BLOCK), F32),
                        pltpu.VMEM((hb, ATT_HEAD_DIM + ATT_DEN_ROWS, MOBA_BLOCK), F32)],
        compiler_params=_params(("arbitrary", "arbitrary", "arbitrary")),
        name="moba_attention",
    )(qaug, kaug, vT, bias_tiles)


def _log_forget(fpre, lb):
    log_sig = jnp.minimum(fpre, 0.0) - jnp.log(1.0 + jnp.exp(-jnp.abs(fpre)))
    a1 = jnp.log(lb)
    a2 = jnp.log1p(-lb) + log_sig
    return jnp.maximum(a1, a2) + jnp.log(1.0 + jnp.exp(-jnp.abs(a1 - a2)))


def _mid_rows(b_scr, b, row, m, ts):
    if m == 1:
        return jnp.where((row & 1) != 0, pltpu.roll(b, 1, 0), b)
    bcast = lambda r, n: jnp.broadcast_to(b_scr[r:r + 1, :], (n, HG_DIM))
    if m == 2:
        lo = jnp.concatenate([bcast(8 * j + 1, 8) for j in range(ts // 8)], axis=0)
        hi = jnp.concatenate([bcast(8 * j + 5, 8) for j in range(ts // 8)], axis=0)
        return jnp.where((row & 4) == 0, lo, hi)
    return jnp.concatenate([bcast(2 * m * j + m - 1, 2 * m) for j in range(ts // (2 * m))], axis=0)


def _halves(q, kk, row, m, ts):
    if m < 8:
        return jnp.where((row & m) != 0, q, kk)
    pieces = [(q if (j & 1) else kk)[j * m:(j + 1) * m] for j in range(ts // m)]
    return jnp.concatenate(pieces, axis=0)


def _hg_kernel(qh_ref, fh_ref, ih_ref, gh_ref, lb_ref, gon_ref, o_ref, sT_ref, b_scr, a_scr, lvl_scr, *, heads):
    ts = qh_ref.shape[0]
    n_levels = ts.bit_length() - 1

    @pl.when((pl.program_id(0) == 0) & (pl.program_id(1) == 0) & (pl.program_id(2) == 0))
    def _():
        t = lax.broadcasted_iota(jnp.int32, (ts, ts), 0)
        s = lax.broadcasted_iota(jnp.int32, (ts, ts), 1)
        x = t ^ s
        lvl = jnp.zeros((ts, ts), jnp.int32)
        for i in range(1, n_levels):
            lvl = lvl + (x >= (1 << i)).astype(jnp.int32)
        lvl_scr[...] = jnp.where(s < t, lvl, -1)

    @pl.when(pl.program_id(2) == 0)
    def _():
        sT_ref[...] = jnp.zeros_like(sT_ref)

    r = lax.broadcasted_iota(jnp.int32, (ts, ts), 0)
    c = lax.broadcasted_iota(jnp.int32, (ts, ts), 1)
    tri = jnp.where(c <= r, 1.0, 0.0).astype(BF16)
    row = lax.broadcasted_iota(jnp.int32, (ts, HG_DIM), 0)

    for hh in range(heads):
        cols = slice(hh * HG_DIM, (hh + 1) * HG_DIM)
        b_ref, a_ref = b_scr.at[hh], a_scr.at[hh]
        lb = lb_ref[:, cols]
        fpre = fh_ref[:, cols]
        qh = qh_ref[:, cols]
        v = ih_ref[:, cols]
        logf = _log_forget(fpre, lb)
        kk = (1.0 - lb) / (1.0 + jnp.exp(fpre))
        q = qh / (1.0 + jnp.exp(-qh))
        b = _dot_exact_lhs(tri, logf) * LOG2E
        b_ref[...] = b
        b_last = b_ref[ts - 1:ts, :]

        a_ref[...] = jnp.zeros((ts, ts), F32)
        for i in range(n_levels):
            m = 1 << i
            e = jnp.exp2(-jnp.abs(b - _mid_rows(b_ref, b, row, m, ts)))
            x = (_halves(q, kk, row, m, ts) * e).astype(BF16)
            p = lax.dot_general(x, x, NT, preferred_element_type=F32)
            if m < 8:
                owned = [(slice(t0, t0 + LANES), slice(t0, t0 + LANES)) for t0 in range(0, ts, LANES)]
            else:
                owned = [(slice(2 * m * j + m, 2 * m * (j + 1)),
                          slice(2 * m * j // LANES * LANES, (2 * m * j // LANES + 1) * LANES))
                         for j in range(ts // (2 * m))]
            for rws, cls in owned:
                a_ref[rws, cls] = jnp.where(lvl_scr[rws, cls] == i, p[rws, cls], a_ref[rws, cls])
        o = _dot(a_ref[...].astype(BF16), v.astype(BF16)) + jnp.sum(q * kk, axis=-1, keepdims=True) * v

        sT = sT_ref[0, hh]
        o = o + lax.dot_general((q * jnp.exp2(b)).astype(BF16), sT.astype(BF16), NT,
                                preferred_element_type=F32)
        k_dec = (kk * jnp.exp2(b_last - b)).astype(BF16)
        sT_ref[0, hh] = sT * jnp.exp2(b_last) + _dot(v.T.astype(BF16), k_dec)

        gh = gh_ref[:, cols]
        o = _rms(o, gon_ref[:, cols]) * (gh / (1.0 + jnp.exp(-gh)))
        o_ref[:, cols] = o.astype(BF16)


def _hg_call(zhg, lb_l, gon_l, batch, seq_len, ts):
    t = zhg.shape[0]
    nt = seq_len // ts
    hb = HG_HEADS_PER_STEP
    groups = HG_HEADS // hb
    part = lambda p: pl.BlockSpec((ts, hb * HG_DIM), lambda b, g, i: (b * nt + i, p * groups + g))
    vec = pl.BlockSpec((1, hb * HG_DIM), lambda b, g, i: (0, g))
    return pl.pallas_call(
        functools.partial(_hg_kernel, heads=hb),
        grid=(batch, groups, nt),
        in_specs=[part(0), part(1), part(2), part(3), vec, vec],
        out_specs=[pl.BlockSpec((ts, hb * HG_DIM), lambda b, g, i: (b * nt + i, g)),
                   pl.BlockSpec((1, hb, HG_DIM, HG_DIM), lambda b, g, i: (b, g, 0, 0))],
        out_shape=[jax.ShapeDtypeStruct((t, HG_WIDTH), BF16),
                   jax.ShapeDtypeStruct((batch, HG_HEADS, HG_DIM, HG_DIM), F32)],
        scratch_shapes=[pltpu.VMEM((hb, ts, HG_DIM), F32), pltpu.VMEM((hb, ts, ts), F32),
                        pltpu.VMEM((ts, ts), jnp.int32)],
        compiler_params=_params(("arbitrary", "arbitrary", "arbitrary")),
        name="hgrn2_prompt",
    )(zhg, zhg, zhg, zhg, lb_l, gon_l)


def _post_kernel(x_ref, oa_ref, oh_ref, ga1_ref, sc2_ref, sh2_ref, ga2_ref, gpm_ref, gpf_ref, gqf_ref,
                 woa_ref, woh_ref, wup_ref, wdn_ref, out_ref, *, ff_chunk):
    x = x_ref[...]
    y = _dot(oa_ref[...], woa_ref[...]) + _dot(oh_ref[...], woh_ref[...])
    x1 = x + ga1_ref[0] * _rms(y, gpm_ref[...])
    h2 = (_rms(x1, gpf_ref[...]) * (1.0 + sc2_ref[0]) + sh2_ref[0]).astype(BF16)
    m = jnp.zeros(x.shape, F32)
    for c in range(D_FF // ff_chunk):
        cols = slice(c * ff_chunk, (c + 1) * ff_chunk)
        u = jnp.square(jnp.maximum(_dot(h2, wup_ref[:, cols]), 0.0)).astype(BF16)
        m = m + _dot(u, wdn_ref[cols, :])
    out_ref[...] = x1 + ga2_ref[0] * _rms(m, gqf_ref[...])


def _post_call(x2, oa, oh, mods, gains, w, rows_per_mod, tm):
    t = x2.shape[0]
    row = lambda i: (i, 0)
    fixed = lambda i: (0, 0)
    mod_rows = mods[0].shape[1]
    mod = pl.BlockSpec((1, mod_rows, D_MODEL), lambda i: ((i * tm) // rows_per_mod, 0, 0))
    gain = pl.BlockSpec((1, D_MODEL), fixed)
    wspec = lambda a: pl.BlockSpec(a.shape, fixed)
    return pl.pallas_call(
        functools.partial(_post_kernel, ff_chunk=1024),
        grid=(t // tm,),
        in_specs=[pl.BlockSpec((tm, D_MODEL), row),
                  pl.BlockSpec((tm, ATT_WIDTH), row),
                  pl.BlockSpec((tm, HG_WIDTH), row),
                  mod, mod, mod, mod, gain, gain, gain,
                  wspec(w["woa"]), wspec(w["woh"]), wspec(w["wup"]), wspec(w["wdn"])],
        out_specs=pl.BlockSpec((tm, D_MODEL), row),
        out_shape=jax.ShapeDtypeStruct((t, D_MODEL), F32),
        compiler_params=_params(("arbitrary",)),
        name="out_proj_mlp",
    )(x2, oa, oh, *mods, *gains, w["woa"], w["woh"], w["wup"], w["wdn"])


def _dec_in_kernel(x_ref, g_ref, sc_ref, sh_ref, w_ref, z_ref):
    h = (_rms(x_ref[...], g_ref[...]) * (1.0 + sc_ref[...]) + sh_ref[...]).astype(BF16)
    z_ref[...] = _dot(h, w_ref[...])


def _dec_in_call(x2, g, sc, sh, w_in_l):
    rows = x2.shape[0]
    n_in = w_in_l.shape[1]
    tn = 512
    full = pl.BlockSpec((rows, D_MODEL), lambda j: (0, 0))
    return pl.pallas_call(
        _dec_in_kernel,
        grid=(n_in // tn,),
        in_specs=[full, pl.BlockSpec((1, D_MODEL), lambda j: (0, 0)), full, full,
                  pl.BlockSpec((D_MODEL, tn), lambda j: (0, j))],
        out_specs=pl.BlockSpec((rows, tn), lambda j: (0, j)),
        out_shape=jax.ShapeDtypeStruct((rows, n_in), F32),
        compiler_params=_params(("arbitrary",)),
        name="in_proj_sample",
    )(x2, g, sc, sh, w_in_l)


def _dec_att_kernel(pt_ref, q_ref, kn_ref, vn_ref, bias_ref, rpe_ref, *refs, pages_per_step, n_blocks):
    k_refs = refs[:pages_per_step]
    v_refs = refs[pages_per_step:2 * pages_per_step]
    o_ref = refs[2 * pages_per_step]
    m_s, l_s, g_s, acc_s = refs[2 * pages_per_step + 1:]
    j = pl.program_id(1)
    last_step = j == pl.num_programs(1) - 1
    bps = pages_per_step // 2
    q = q_ref[0] * Q_SCALE
    c_far = rpe_ref[:, :, RPE_BUCKETS - 1:RPE_BUCKETS] * LOG2E
    stat = (1, ATT_HEADS, 1, LANES)

    scores = lambda k_page: jnp.sum(k_page * q, axis=1, keepdims=True)
    lane_sum = lambda x: jnp.sum(x, axis=-1, keepdims=True)
    lane_max = lambda x: jnp.max(x, axis=-1, keepdims=True)

    for blk in range(bps):
        s0 = scores(k_refs[2 * blk][0, 0])
        s1 = scores(k_refs[2 * blk + 1][0, 0])
        g = lane_sum(s0 + s1)
        s0 = s0 + c_far
        if blk == bps - 1:
            s1 = s1 + jnp.where(last_step, bias_ref[...], c_far)
        else:
            s1 = s1 + c_far
        m = jnp.maximum(lane_max(s0), lane_max(s1))
        p0 = jnp.exp2(s0 - m)
        p1 = jnp.exp2(s1 - m)
        row = j * bps + blk
        m_s[pl.ds(row, 1)] = jnp.broadcast_to(m, stat)
        l_s[pl.ds(row, 1)] = jnp.broadcast_to(lane_sum(p0 + p1), stat)
        g_s[pl.ds(row, 1)] = jnp.broadcast_to(g, stat)
        acc_s[pl.ds(row, 1)] = (p0 * v_refs[2 * blk][0, 0] + p1 * v_refs[2 * blk + 1][0, 0])[None]

    @pl.when(last_step)
    def _():
        s_new = scores(kn_ref[0]) + rpe_ref[:, :, 0:1] * LOG2E
        lane = lax.broadcasted_iota(jnp.int32, (ATT_HEADS, ATT_HEAD_DIM, LANES), 2)
        m_s[n_blocks:n_blocks + 1] = s_new[None]
        l_s[n_blocks:n_blocks + 1] = jnp.ones(stat, F32)
        g_s[n_blocks:n_blocks + 1] = jnp.zeros(stat, F32)
        acc_s[n_blocks:n_blocks + 1] = jnp.where(lane == 0, vn_ref[0], 0.0)[None]
        rows = n_blocks + 1
        n = lax.broadcasted_iota(jnp.int32, m_s.shape, 0)
        gate = jnp.where(n < n_blocks, g_s[...], -jnp.inf)
        sel = _top_blocks(gate, n, rows, lambda jj: jj < n_blocks) | (n == n_blocks)
        mm = jnp.where(sel, m_s[...], -jnp.inf)
        w = jnp.exp2(mm - jnp.max(mm, axis=0, keepdims=True))
        den = jnp.sum(w * l_s[...], axis=0)
        num = lane_sum(jnp.sum(w * acc_s[...], axis=0))
        o_ref[0] = jnp.broadcast_to(num / den[:, :, 0:1], o_ref.shape[1:]).astype(BF16)


def _dec_att_call(page_table, q4, kn4, vn4, dec_bias, rpe_heads, cache_kT, cache_vT, layer, pages_per_step):
    n_seq, n_pages = page_table.shape
    n_blocks = n_pages * PAGE_SIZE // MOBA_BLOCK
    steps = n_pages // pages_per_step
    rows = n_blocks + 1
    tile = (ATT_HEADS, ATT_HEAD_DIM, LANES)
    vec = pl.BlockSpec((1, *tile), lambda b, j, pt: (b, 0, 0, 0))

    def page_spec(p):
        return pl.BlockSpec((1, 1, ATT_HEADS, ATT_HEAD_DIM, PAGE_SIZE),
                            lambda b, j, pt: (layer, pt[b, j * pages_per_step + p], 0, 0, 0))

    grid_spec = pltpu.PrefetchScalarGridSpec(
        num_scalar_prefetch=1,
        grid=(n_seq, steps),
        in_specs=[vec, vec, vec,
                  pl.BlockSpec((ATT_HEADS, 1, PAGE_SIZE), lambda b, j, pt: (0, 0, 0)),
                  pl.BlockSpec((ATT_HEADS, 1, RPE_BUCKETS), lambda b, j, pt: (0, 0, 0))]
                 + [page_spec(p) for p in range(pages_per_step)]
                 + [page_spec(p) for p in range(pages_per_step)],
        out_specs=vec,
        scratch_shapes=[pltpu.VMEM((rows, ATT_HEADS, 1, LANES), F32),
                        pltpu.VMEM((rows, ATT_HEADS, 1, LANES), F32),
                        pltpu.VMEM((rows, ATT_HEADS, 1, LANES), F32),
                        pltpu.VMEM((rows, *tile), F32)],
    )
    return pl.pallas_call(
        functools.partial(_dec_att_kernel, pages_per_step=pages_per_step, n_blocks=n_blocks),
        grid_spec=grid_spec,
        out_shape=jax.ShapeDtypeStruct((n_seq, *tile), BF16),
        compiler_params=_params(("arbitrary", "arbitrary")),
        name="moba_attention_sample",
    )(page_table, q4, kn4, vn4, dec_bias, rpe_heads,
      *([cache_kT] * pages_per_step), *([cache_vT] * pages_per_step))


def _dec_hg_kernel(qh_ref, fh_ref, ih_ref, gh_ref, lb_ref, gon_ref, s_ref, o_ref, so_ref):
    r = lax.broadcasted_iota(jnp.int32, (HG_DIM, HG_DIM), 0)
    c = lax.broadcasted_iota(jnp.int32, (HG_DIM, HG_DIM), 1)
    eye = r == c

    def column(x):
        return jnp.sum(jnp.where(eye, jnp.broadcast_to(x, (HG_DIM, HG_DIM)), 0.0), axis=1, keepdims=True)

    for h in range(HG_HEADS):
        cols = slice(h * HG_DIM, (h + 1) * HG_DIM)
        lb = lb_ref[:, cols]
        fpre = fh_ref[0][:, cols]
        qh = qh_ref[0][:, cols]
        gh = gh_ref[0][:, cols]
        v = ih_ref[0][:, cols]
        f = jnp.exp(_log_forget(fpre, lb))
        kk = (1.0 - lb) / (1.0 + jnp.exp(fpre))
        q = qh / (1.0 + jnp.exp(-qh))
        s_new = column(f) * s_ref[0, h] + column(kk) * v
        so_ref[0, h] = s_new
        o = jnp.sum(column(q) * s_new, axis=0, keepdims=True)
        o = _rms(o, gon_ref[:, cols]) * (gh / (1.0 + jnp.exp(-gh)))
        o_ref[0, :, cols] = o.astype(BF16)


def _dec_hg_call(qh3, fh3, ih3, gh3, lb_l, gon_l, state_l):
    n_seq = qh3.shape[0]
    vec = pl.BlockSpec((1, 1, HG_WIDTH), lambda b: (b, 0, 0))
    par = pl.BlockSpec((1, HG_WIDTH), lambda b: (0, 0))
    st = pl.BlockSpec((1, HG_HEADS, HG_DIM, HG_DIM), lambda b: (b, 0, 0, 0))
    return pl.pallas_call(
        _dec_hg_kernel,
        grid=(n_seq,),
        in_specs=[vec, vec, vec, vec, par, par, st],
        out_specs=[vec, st],
        out_shape=[jax.ShapeDtypeStruct((n_seq, 1, HG_WIDTH), BF16),
                   jax.ShapeDtypeStruct(state_l.shape, F32)],
        compiler_params=_params(("arbitrary",)),
        name="hgrn2_sample",
    )(qh3, fh3, ih3, gh3, lb_l, gon_l, state_l)


def _layer_weights(w_in, w_out, w_up, w_down, l):
    a = ATT_WIDTH
    wi = w_in[l]
    wq, wk, wv, whg = wi[:, :a], wi[:, a:2 * a], wi[:, 2 * a:3 * a], wi[:, 3 * a:]
    wkp = jnp.pad(wk.reshape(D_MODEL, ATT_HEADS, ATT_HEAD_DIM),
                  ((0, 0), (0, 0), (0, HEAD_PAD - ATT_HEAD_DIM))).reshape(D_MODEL, ATT_HEADS * HEAD_PAD)
    return {"wqT": wq.T, "wkT": wk.T, "wvT": wv.T, "wkp": wkp, "whg": whg, "w_in": wi,
            "woa": w_out[l, :a], "woh": w_out[l, a:], "wup": w_up[l], "wdn": w_down[l]}


def kernel(x_prompt, x_sample, c_prompt, c_sample, cache_k, cache_v, state_hgrn, page_table, w_ada, b_ada,
           g_pre_mix, g_post_mix, g_pre_ffn, g_post_ffn, w_in, lb_param, g_onorm, w_out, w_up, w_down,
           rpe_table):
    batch, seq_len, _ = x_prompt.shape
    n_seq, dec_seq, _ = x_sample.shape
    n_pages = page_table.shape[1]
    assert dec_seq == 1 and seq_len % MOBA_BLOCK == 0 and (n_pages * PAGE_SIZE) % MOBA_BLOCK == 0
    assert seq_len // MOBA_BLOCK <= GATE_SLOTS
    tm = 512 if seq_len % 512 == 0 else MOBA_BLOCK
    pages_per_step = next(p for p in (16, 8, 4, 2) if n_pages % p == 0)
    a = ATT_WIDTH
    hd = (ATT_HEADS, ATT_HEAD_DIM)

    w_in_b, w_out_b = w_in.astype(BF16), w_out.astype(BF16)
    w_up_b, w_down_b = w_up.astype(BF16), w_down.astype(BF16)

    n_c = batch + n_seq
    c_rows = ((n_c + 7) // 8) * 8
    c_all = jnp.pad(jnp.concatenate([c_prompt, c_sample], axis=0), ((0, c_rows - n_c), (0, 0)))
    mod = _ada_call(c_all, w_ada, b_ada)
    lb = _lb_call(lb_param)
    rpe_flat = rpe_table.T.reshape(-1)
    rpe_heads = rpe_table.T.reshape(ATT_HEADS, 1, RPE_BUCKETS)
    bias_tiles = _bias_tile_call(rpe_flat)
    dec_bias = _dec_bias_call(rpe_heads)
    cache_kT = jnp.transpose(cache_k, (0, 1, 3, 4, 2))
    cache_vT = jnp.transpose(cache_v, (0, 1, 3, 4, 2))

    xp = x_prompt.reshape(batch * seq_len, D_MODEL)
    xs = x_sample.reshape(n_seq, D_MODEL)
    k_p, v_p, s_p, k_s, v_s, s_s = [], [], [], [], [], []
    for l in range(DEPTH):
        w = _layer_weights(w_in_b, w_out_b, w_up_b, w_down_b, l)
        row = lambda arr: arr[l].reshape(1, -1)
        gains = (row(g_post_mix), row(g_pre_ffn), row(g_post_ffn))
        lb_l, gon_l = row(lb), row(g_onorm)

        mp = mod[l, :, :batch].reshape(6, batch, 1, D_MODEL)
        qT, kT, vT, vTb, kaug, kmean, zhg = _in_proj_call(xp, row(g_pre_mix), mp[1], mp[0], w, batch,
                                                          seq_len, tm)
        qaug = _gate_call(rpe_flat, kmean.reshape(batch, seq_len // MOBA_BLOCK, -1), qT, batch, seq_len)
        oT = _att_call(qaug, kaug, vTb, bias_tiles, batch, seq_len)
        o_att = jnp.swapaxes(oT, 1, 2).reshape(batch * seq_len, a)
        o_hg, sT = _hg_call(zhg, lb_l, gon_l, batch, seq_len, MOBA_BLOCK)
        xp = _post_call(xp, o_att, o_hg, (mp[2], mp[4], mp[3], mp[5]), gains, w, seq_len, tm)
        k_p.append(kT)
        v_p.append(vT)
        s_p.append(jnp.swapaxes(sT, -1, -2))

        ms = mod[l, :, batch:n_c]
        z = _dec_in_call(xs, row(g_pre_mix), ms[1], ms[0], w["w_in"])
        part = lambda i: z[:, i * a:(i + 1) * a]
        wide = lambda i: jnp.broadcast_to(part(i).reshape(n_seq, *hd, 1), (n_seq, *hd, LANES))
        o4 = _dec_att_call(page_table, wide(0), wide(1), wide(2), dec_bias, rpe_heads, cache_kT, cache_vT,
                           l, pages_per_step)
        o_att_s = o4[..., 0].reshape(n_seq, a)
        row3 = lambda i: part(i).reshape(n_seq, 1, a)
        o_hgs, s_new = _dec_hg_call(row3(3), row3(4), row3(5), row3(6), lb_l, gon_l, state_hgrn[l])
        ms4 = ms.reshape(6, 1, n_seq, D_MODEL)
        xs = _post_call(xs, o_att_s, o_hgs.reshape(n_seq, a), (ms4[2], ms4[4], ms4[3], ms4[5]), gains, w,
                        n_seq, n_seq)
        k_s.append(part(1))
        v_s.append(part(2))
        s_s.append(s_new)

    rows_out = lambda parts: jnp.transpose(jnp.stack(parts).reshape(DEPTH, batch, *hd, seq_len),
                                           (0, 1, 4, 2, 3))
    return (xp.reshape(batch, seq_len, D_MODEL),
            xs.reshape(n_seq, 1, D_MODEL),
            rows_out(k_p),
            rows_out(v_p),
            jnp.stack(s_p),
            jnp.stack(k_s).reshape(DEPTH, n_seq, 1, *hd),
            jnp.stack(v_s).reshape(DEPTH, n_seq, 1, *hd),
            jnp.stack(s_s))
```

```python
import functools
import math

import numpy as np
import jax
import jax.numpy as jnp
from jax import lax
from jax.experimental import pallas as pl
from jax.experimental.pallas import tpu as pltpu

F32 = jnp.float32
BF16 = jnp.bfloat16

D_MODEL = 1024
DEPTH = 4
ATT_HEADS = 8
ATT_HEAD_DIM = 64
ATT_WIDTH = ATT_HEADS * ATT_HEAD_DIM
MOBA_BLOCK = 256
MOBA_TOPK = 3
PAGE_SIZE = 128
RPE_BUCKETS = 32
RPE_MAX_DIST = 128
HG_HEADS = 4
HG_DIM = 128
HG_WIDTH = HG_HEADS * HG_DIM
D_FF = 4 * D_MODEL
EPS = 1e-6

LANES = 128
HEAD_PAD = 128
GATE_SLOTS = 32
ATT_HEADS_PER_STEP = 4
ATT_SPAN = 4
ATT_DEN_ROWS = 16
HG_HEADS_PER_STEP = 4
LOG2E = 1.4426950408889634
Q_SCALE = ATT_HEAD_DIM ** -0.5 * LOG2E
NEG = -30000.0
VMEM_LIMIT = 56 * 1024 * 1024

NT = (((1,), (1,)), ((), ()))


def _bucket_starts():
    max_exact = RPE_BUCKETS // 2
    d = np.arange(0, RPE_MAX_DIST + 1)
    dd = np.maximum(d, max_exact).astype(np.float32)
    large = max_exact + (np.log(dd / np.float32(max_exact)) / np.float32(math.log(RPE_MAX_DIST / max_exact))
                         * np.float32(RPE_BUCKETS - max_exact)).astype(np.int32)
    large = np.minimum(large, RPE_BUCKETS - 1)
    b = np.where(d < max_exact, d, large)
    return tuple(int(np.argmax(b >= k)) for k in range(RPE_BUCKETS))


BUCKET_STARTS = _bucket_starts()


def _params(sem):
    return pltpu.CompilerParams(dimension_semantics=sem, vmem_limit_bytes=VMEM_LIMIT)


def _rms(x, g):
    return x * lax.rsqrt(jnp.mean(x * x, axis=-1, keepdims=True) + EPS) * g


def _dot(a, b):
    return jnp.dot(a, b, preferred_element_type=F32)


def _dot_exact_lhs(a, x):
    hi = x.astype(BF16)
    r = x - hi.astype(F32)
    mid = r.astype(BF16)
    lo = (r - mid.astype(F32)).astype(BF16)
    return _dot(a, hi) + _dot(a, mid) + _dot(a, lo)


def _top_blocks(gate, n, sentinel, hit_ok):
    sel = jnp.zeros(gate.shape, jnp.bool_)
    for j in range(MOBA_TOPK):
        mx = jnp.max(gate, axis=0, keepdims=True)
        idx = jnp.min(jnp.where(gate == mx, n, sentinel), axis=0, keepdims=True)
        hit = n == idx
        sel = sel | (hit & hit_ok(j))
        gate = jnp.where(hit, -jnp.inf, gate)
    return sel


def _ada_kernel(c_ref, w_ref, b_ref, o_ref):
    c = c_ref[...]
    act = (c / (1.0 + jnp.exp(-c))).astype(BF16)
    o_ref[0, 0] = _dot(act, w_ref[0].astype(BF16)) + b_ref[0, 0]


def _ada_call(c_all, w_ada, b_ada):
    rows = c_all.shape[0]
    return pl.pallas_call(
        _ada_kernel,
        grid=(DEPTH, 6),
        in_specs=[pl.BlockSpec((rows, D_MODEL), lambda l, j: (0, 0)),
                  pl.BlockSpec((1, D_MODEL, D_MODEL), lambda l, j: (l, 0, j)),
                  pl.BlockSpec((1, 1, 1, D_MODEL), lambda l, j: (l, j, 0, 0))],
        out_specs=pl.BlockSpec((1, 1, rows, D_MODEL), lambda l, j: (l, j, 0, 0)),
        out_shape=jax.ShapeDtypeStruct((DEPTH, 6, rows, D_MODEL), F32),
        compiler_params=_params(("arbitrary", "arbitrary")),
        name="ada_mod",
    )(c_all, w_ada, b_ada.reshape(DEPTH, 6, 1, D_MODEL))


def _lb_kernel(p_ref, o_ref):
    p = p_ref[...]
    e = jnp.exp(p - jnp.max(p, axis=0, keepdims=True))
    sm = e / jnp.sum(e, axis=0, keepdims=True)
    acc = jnp.zeros((1, HG_WIDTH), F32)
    for l in range(DEPTH):
        o_ref[l:l + 1, :] = acc
        if l + 1 < DEPTH:
            acc = acc + sm[l + 1:l + 2, :]


def _lb_call(lb_param):
    return pl.pallas_call(
        _lb_kernel,
        out_shape=jax.ShapeDtypeStruct((DEPTH, HG_WIDTH), F32),
        name="hgrn_lower_bounds",
    )(lb_param)


def _bias_lookup(d, table):
    val = table(RPE_BUCKETS - 1)
    for b in range(RPE_BUCKETS - 2, -1, -1):
        val = jnp.where(d < BUCKET_STARTS[b + 1], table(b), val)
    return val * LOG2E


def _bias_tile_kernel(rpe_ref, o_ref):
    h = pl.program_id(0)
    kk = lax.broadcasted_iota(jnp.int32, (MOBA_BLOCK, MOBA_BLOCK), 0)
    qq = lax.broadcasted_iota(jnp.int32, (MOBA_BLOCK, MOBA_BLOCK), 1)
    d = qq - kk
    table = lambda b: rpe_ref[h * RPE_BUCKETS + b]
    o_ref[0, 0] = _bias_lookup(d + MOBA_BLOCK, table)
    o_ref[0, 1] = jnp.where(d >= 0, _bias_lookup(d, table), NEG)


def _bias_tile_call(rpe_flat):
    return pl.pallas_call(
        _bias_tile_kernel,
        grid=(ATT_HEADS,),
        in_specs=[pl.BlockSpec(memory_space=pltpu.SMEM)],
        out_specs=pl.BlockSpec((1, 2, MOBA_BLOCK, MOBA_BLOCK), lambda h: (h, 0, 0, 0)),
        out_shape=jax.ShapeDtypeStruct((ATT_HEADS, 2, MOBA_BLOCK, MOBA_BLOCK), F32),
        compiler_params=_params(("arbitrary",)),
        name="rpe_bias_tiles",
    )(rpe_flat)


def _dec_bias_kernel(rpe_ref, o_ref):
    assert BUCKET_STARTS[-1] <= PAGE_SIZE + 1
    d = PAGE_SIZE - lax.broadcasted_iota(jnp.int32, (ATT_HEADS, 1, PAGE_SIZE), 2)
    o_ref[...] = _bias_lookup(d, lambda b: rpe_ref[:, :, b:b + 1])


def _dec_bias_call(rpe_heads):
    return pl.pallas_call(
        _dec_bias_kernel,
        out_shape=jax.ShapeDtypeStruct((ATT_HEADS, 1, PAGE_SIZE), F32),
        name="rpe_bias_decode",
    )(rpe_heads)


def _in_proj_kernel(x_ref, g_ref, sc_ref, sh_ref, wqT_ref, wkT_ref, wvT_ref, wkp_ref, whg_ref,
                    qT_ref, kT_ref, vT_ref, vTb_ref, kaug_ref, kmean_ref, zhg_ref, *, blocks_per_seq):
    x = x_ref[...]
    tm = x.shape[0]
    h = (_rms(x, g_ref[...]) * (1.0 + sc_ref[0]) + sh_ref[0]).astype(BF16)
    transposed = lambda w_ref: lax.dot_general(w_ref[...], h, NT, preferred_element_type=F32)
    qT_ref[0] = (transposed(wqT_ref) * Q_SCALE).astype(BF16)
    kT_ref[0] = transposed(wkT_ref)
    vT = transposed(wvT_ref)
    vT_ref[0] = vT
    vTb_ref[0] = vT.astype(BF16)
    zhg_ref[...] = _dot(h, whg_ref[...])
    kp = _dot(h, wkp_ref[...])
    nb = tm // MOBA_BLOCK
    for r in range(nb):
        kmean_ref[r] = jnp.mean(kp[r * MOBA_BLOCK:(r + 1) * MOBA_BLOCK], axis=0, keepdims=True)
    row = lax.broadcasted_iota(jnp.int32, kp.shape, 0)
    lane = lax.broadcasted_iota(jnp.int32, kp.shape, 1) % HEAD_PAD
    blk = (pl.program_id(0) * nb + row // MOBA_BLOCK) % blocks_per_seq
    onehot = (lane == blk + ATT_HEAD_DIM) | (lane == blk + ATT_HEAD_DIM + GATE_SLOTS)
    kaug_ref[...] = jnp.where(onehot, 1.0, kp).astype(BF16)


def _in_proj_call(x2, g, sc, sh, w, batch, seq_len, tm):
    t = x2.shape[0]
    tps = seq_len // tm
    nbt = tm // MOBA_BLOCK
    hp = ATT_HEADS * HEAD_PAD
    row = lambda i: (i, 0)
    fixed = lambda i: (0, 0)
    mod = lambda i: (i // tps, 0, 0)
    tr = pl.BlockSpec((1, ATT_WIDTH, tm), lambda i: (i // tps, 0, i % tps))
    wspec = lambda a: pl.BlockSpec(a.shape, fixed)
    tshape = lambda dt: jax.ShapeDtypeStruct((batch, ATT_WIDTH, seq_len), dt)
    return pl.pallas_call(
        functools.partial(_in_proj_kernel, blocks_per_seq=seq_len // MOBA_BLOCK),
        grid=(t // tm,),
        in_specs=[pl.BlockSpec((tm, D_MODEL), row),
                  pl.BlockSpec((1, D_MODEL), fixed),
                  pl.BlockSpec((1, 1, D_MODEL), mod),
                  pl.BlockSpec((1, 1, D_MODEL), mod),
                  wspec(w["wqT"]), wspec(w["wkT"]), wspec(w["wvT"]), wspec(w["wkp"]), wspec(w["whg"])],
        out_specs=[tr, tr, tr, tr,
                   pl.BlockSpec((tm, hp), row),
                   pl.BlockSpec((nbt, 1, hp), lambda i: (i, 0, 0)),
                   pl.BlockSpec((tm, 4 * HG_WIDTH), row)],
        out_shape=[tshape(BF16), tshape(F32), tshape(F32), tshape(BF16),
                   jax.ShapeDtypeStruct((t, hp), BF16),
                   jax.ShapeDtypeStruct((t // MOBA_BLOCK, 1, hp), F32),
                   jax.ShapeDtypeStruct((t, 4 * HG_WIDTH), F32)],
        compiler_params=_params(("arbitrary",)),
        name="in_proj",
    )(x2, g, sc, sh, w["wqT"], w["wkT"], w["wvT"], w["wkp"], w["whg"])


def _gate_kernel(rpe_ref, kmean_ref, qT_ref, o_ref, *, nbs):
    own = pl.program_id(1)
    tq = qT_ref.shape[2]
    n = lax.broadcasted_iota(jnp.int32, (nbs, tq), 0)
    far = n <= own - 2
    for h in range(ATT_HEADS):
        q = qT_ref[0, h * ATT_HEAD_DIM:(h + 1) * ATT_HEAD_DIM, :]
        km = kmean_ref[0][:, h * HEAD_PAD:h * HEAD_PAD + ATT_HEAD_DIM].astype(BF16)
        gate = jnp.where(n < own, _dot(km, q), -jnp.inf)
        sel = _top_blocks(gate, n, nbs, lambda j: j < own)
        c = jnp.full((nbs, tq), rpe_ref[h * RPE_BUCKETS + RPE_BUCKETS - 1] * LOG2E, F32)
        c_hi = c.astype(BF16).astype(F32)
        c_lo = c - c_hi
        p_hi = jnp.where(sel, jnp.where(far, c_hi, 0.0), NEG)
        p_hi = jnp.where(n >= own, 0.0, p_hi)
        p_lo = jnp.where(sel & far, c_lo, 0.0)
        base = h * HEAD_PAD
        o_ref[0, base:base + ATT_HEAD_DIM, :] = q
        o_ref[0, base + ATT_HEAD_DIM:base + HEAD_PAD, :] = jnp.zeros((HEAD_PAD - ATT_HEAD_DIM, tq), BF16)
        o_ref[0, base + ATT_HEAD_DIM:base + ATT_HEAD_DIM + nbs, :] = p_hi.astype(BF16)
        o_ref[0, base + ATT_HEAD_DIM + GATE_SLOTS:base + ATT_HEAD_DIM + GATE_SLOTS + nbs, :] = p_lo.astype(BF16)


def _gate_call(rpe_flat, kmean, qT, batch, seq_len):
    nbs = seq_len // MOBA_BLOCK
    hp = ATT_HEADS * HEAD_PAD
    return pl.pallas_call(
        functools.partial(_gate_kernel, nbs=nbs),
        grid=(batch, nbs),
        in_specs=[pl.BlockSpec(memory_space=pltpu.SMEM),
                  pl.BlockSpec((1, nbs, hp), lambda b, i: (b, 0, 0)),
                  pl.BlockSpec((1, ATT_WIDTH, MOBA_BLOCK), lambda b, i: (b, 0, i))],
        out_specs=pl.BlockSpec((1, hp, MOBA_BLOCK), lambda b, i: (b, 0, i)),
        out_shape=jax.ShapeDtypeStruct((batch, hp, seq_len), BF16),
        compiler_params=_params(("arbitrary", "arbitrary")),
        name="moba_gate",
    )(rpe_flat, kmean, qT)


def _att_kernel(q_ref, k_ref, vT_ref, bias_ref, o_ref, s_a, s_b, x_a, x_b, p_scr, m_scr, a_scr, acc_scr, *,
                heads, n_key_blocks):
    qi = pl.program_id(2)
    ones = jnp.ones((ATT_DEN_ROWS, ATT_SPAN * MOBA_BLOCK), BF16)

    def scores(first_block, n_blocks, near, s_scr, x_scr):
        keys = n_blocks * MOBA_BLOCK
        start = pl.multiple_of(first_block * MOBA_BLOCK, MOBA_BLOCK)
        for h in range(heads):
            q = q_ref[0, h * HEAD_PAD:(h + 1) * HEAD_PAD, :]
            s = _dot(k_ref[pl.ds(start, keys), h * HEAD_PAD:(h + 1) * HEAD_PAD], q)
            if near:
                s = s + bias_ref[h, (2 - n_blocks) * MOBA_BLOCK:2 * MOBA_BLOCK, :]
            s_scr[h, 0:keys, :] = s
            x_scr[h] = jnp.max(s, axis=0, keepdims=True)

    def absorb(first_block, n_blocks, s_scr, x_scr):
        keys = n_blocks * MOBA_BLOCK
        start = pl.multiple_of(first_block * MOBA_BLOCK, MOBA_BLOCK)
        for h in range(heads):
            m = m_scr[h]
            m_new = jnp.maximum(m, x_scr[h])
            a_scr[h] = jnp.exp2(m - m_new)
            m_scr[h] = m_new
            p_scr[h, 0:keys, :] = jnp.exp2(s_scr[h, 0:keys, :] - m_new).astype(BF16)
        for h in range(heads):
            v = vT_ref[0, h * ATT_HEAD_DIM:(h + 1) * ATT_HEAD_DIM, pl.ds(start, keys)]
            v1 = jnp.concatenate([v, ones[:, 0:keys]], axis=0)
            acc_scr[h] = a_scr[h] * acc_scr[h] + _dot(v1, p_scr[h, 0:keys, :])

    def span(first_block, n_blocks, near):
        scores(first_block, n_blocks, near, s_a, x_a)
        absorb(first_block, n_blocks, s_a, x_a)

    m_scr[...] = jnp.full(m_scr.shape, -jnp.inf, F32)
    acc_scr[...] = jnp.zeros(acc_scr.shape, F32)

    @pl.when(qi == 0)
    def _():
        span(0, 1, True)

    @pl.when(qi >= 1)
    def _():
        span(qi - 1, 2, True)

    n_far = jnp.maximum(qi - 1, 0)
    n_full = n_far // ATT_SPAN
    ahead = lambda i: jnp.minimum(i * ATT_SPAN, n_key_blocks - ATT_SPAN)

    @pl.when(n_full > 0)
    def _():
        scores(0, ATT_SPAN, False, s_a, x_a)

    @pl.loop(0, n_full // 2)
    def _(j):
        scores(ahead(2 * j + 1), ATT_SPAN, False, s_b, x_b)
        absorb(2 * j * ATT_SPAN, ATT_SPAN, s_a, x_a)
        scores(ahead(2 * j + 2), ATT_SPAN, False, s_a, x_a)
        absorb((2 * j + 1) * ATT_SPAN, ATT_SPAN, s_b, x_b)

    @pl.when(n_full % 2 == 1)
    def _():
        absorb((n_full - 1) * ATT_SPAN, ATT_SPAN, s_a, x_a)

    done = n_full * ATT_SPAN
    size = ATT_SPAN // 2
    while size >= 1:
        @pl.when((n_far - done) & size != 0)
        def _(done=done, size=size):
            span(done, size, False)
        done = done + ((n_far - done) & size)
        size //= 2

    for h in range(heads):
        acc = acc_scr[h]
        den = acc[ATT_HEAD_DIM:ATT_HEAD_DIM + 1, :]
        o_ref[0, h * ATT_HEAD_DIM:(h + 1) * ATT_HEAD_DIM, :] = (acc[0:ATT_HEAD_DIM, :] / den).astype(BF16)


def _att_call(qaug, kaug, vT, bias_tiles, batch, seq_len):
    nq = seq_len // MOBA_BLOCK
    hb = ATT_HEADS_PER_STEP
    return pl.pallas_call(
        functools.partial(_att_kernel, heads=hb, n_key_blocks=nq),
        grid=(batch, ATT_HEADS // hb, nq),
        in_specs=[pl.BlockSpec((1, hb * HEAD_PAD, MOBA_BLOCK), lambda b, h, i: (b, h, i)),
                  pl.BlockSpec((seq_len, hb * HEAD_PAD), lambda b, h, i: (b, h)),
                  pl.BlockSpec((1, hb * ATT_HEAD_DIM, seq_len), lambda b, h, i: (b, h, 0)),
                  pl.BlockSpec((hb, 2 * MOBA_BLOCK, MOBA_BLOCK), lambda b, h, i: (h, 0, 0))],
        out_specs=pl.BlockSpec((1, hb * ATT_HEAD_DIM, MOBA_BLOCK), lambda b, h, i: (b, h, i)),
        out_shape=jax.ShapeDtypeStruct((batch, ATT_WIDTH, seq_len), BF16),
        scratch_shapes=[pltpu.VMEM((hb, ATT_SPAN * MOBA_BLOCK, MOBA_BLOCK), F32),
                        pltpu.VMEM((hb, ATT_SPAN * MOBA_BLOCK, MOBA_BLOCK), F32),
                        pltpu.VMEM((hb, 1, MOBA_BLOCK), F32),
                        pltpu.VMEM((hb, 1, MOBA_BLOCK), F32),
                        pltpu.VMEM((hb, ATT_SPAN * MOBA_BLOCK, MOBA_BLOCK), BF16),
                        pltpu.VMEM((hb, 1, MOBA_BLOCK), F32),
                        pltpu.VMEM((hb, 1, MOBA_BLOCK), F32),
                        pltpu.VMEM((hb, ATT_HEAD_DIM + ATT_DEN_ROWS, MOBA_BLOCK), F32)],
        compiler_params=_params(("arbitrary", "arbitrary", "arbitrary")),
        name="moba_attention",
    )(qaug, kaug, vT, bias_tiles)


def _log_forget(fpre, lb):
    log_sig = jnp.minimum(fpre, 0.0) - jnp.log(1.0 + jnp.exp(-jnp.abs(fpre)))
    a1 = jnp.log(lb)
    a2 = jnp.log1p(-lb) + log_sig
    return jnp.maximum(a1, a2) + jnp.log(1.0 + jnp.exp(-jnp.abs(a1 - a2)))


def _mid_rows(b_scr, b, row, m, ts):
    if m == 1:
        return jnp.where((row & 1) != 0, pltpu.roll(b, 1, 0), b)
    bcast = lambda r, n: jnp.broadcast_to(b_scr[r:r + 1, :], (n, HG_DIM))
    if m == 2:
        lo = jnp.concatenate([bcast(8 * j + 1, 8) for j in range(ts // 8)], axis=0)
        hi = jnp.concatenate([bcast(8 * j + 5, 8) for j in range(ts // 8)], axis=0)
        return jnp.where((row & 4) == 0, lo, hi)
    return jnp.concatenate([bcast(2 * m * j + m - 1, 2 * m) for j in range(ts // (2 * m))], axis=0)


def _halves(q, kk, row, m, ts):
    if m < 8:
        return jnp.where((row & m) != 0, q, kk)
    pieces = [(q if (j & 1) else kk)[j * m:(j + 1) * m] for j in range(ts // m)]
    return jnp.concatenate(pieces, axis=0)


def _hg_kernel(qh_ref, fh_ref, ih_ref, gh_ref, lb_ref, gon_ref, o_ref, sT_ref, b_scr, a_scr, lvl_scr, *, heads):
    ts = qh_ref.shape[0]
    n_levels = ts.bit_length() - 1

    @pl.when((pl.program_id(0) == 0) & (pl.program_id(1) == 0) & (pl.program_id(2) == 0))
    def _():
        t = lax.broadcasted_iota(jnp.int32, (ts, ts), 0)
        s = lax.broadcasted_iota(jnp.int32, (ts, ts), 1)
        x = t ^ s
        lvl = jnp.zeros((ts, ts), jnp.int32)
        for i in range(1, n_levels):
            lvl = lvl + (x >= (1 << i)).astype(jnp.int32)
        lvl_scr[...] = jnp.where(s < t, lvl, -1)

    @pl.when(pl.program_id(2) == 0)
    def _():
        sT_ref[...] = jnp.zeros_like(sT_ref)

    r = lax.broadcasted_iota(jnp.int32, (ts, ts), 0)
    c = lax.broadcasted_iota(jnp.int32, (ts, ts), 1)
    tri = jnp.where(c <= r, 1.0, 0.0).astype(BF16)
    row = lax.broadcasted_iota(jnp.int32, (ts, HG_DIM), 0)

    for hh in range(heads):
        cols = slice(hh * HG_DIM, (hh + 1) * HG_DIM)
        b_ref, a_ref = b_scr.at[hh], a_scr.at[hh]
        lb = lb_ref[:, cols]
        fpre = fh_ref[:, cols]
        qh = qh_ref[:, cols]
        v = ih_ref[:, cols]
        logf = _log_forget(fpre, lb)
        kk = (1.0 - lb) / (1.0 + jnp.exp(fpre))
        q = qh / (1.0 + jnp.exp(-qh))
        b = _dot_exact_lhs(tri, logf) * LOG2E
        b_ref[...] = b
        b_last = b_ref[ts - 1:ts, :]

        a_ref[...] = jnp.zeros((ts, ts), F32)
        for i in range(n_levels):
            m = 1 << i
            e = jnp.exp2(-jnp.abs(b - _mid_rows(b_ref, b, row, m, ts)))
            x = (_halves(q, kk, row, m, ts) * e).astype(BF16)
            p = lax.dot_general(x, x, NT, preferred_element_type=F32)
            if m < 8:
                owned = [(slice(t0, t0 + LANES), slice(t0, t0 + LANES)) for t0 in range(0, ts, LANES)]
            else:
                owned = [(slice(2 * m * j + m, 2 * m * (j + 1)),
                          slice(2 * m * j // LANES * LANES, (2 * m * j // LANES + 1) * LANES))
                         for j in range(ts // (2 * m))]
            for rws, cls in owned:
                a_ref[rws, cls] = jnp.where(lvl_scr[rws, cls] == i, p[rws, cls], a_ref[rws, cls])
        o = _dot(a_ref[...].astype(BF16), v.astype(BF16)) + jnp.sum(q * kk, axis=-1, keepdims=True) * v

        sT = sT_ref[0, hh]
        o = o + lax.dot_general((q * jnp.exp2(b)).astype(BF16), sT.astype(BF16), NT,
                                preferred_element_type=F32)
        k_dec = (kk * jnp.exp2(b_last - b)).astype(BF16)
        sT_ref[0, hh] = sT * jnp.exp2(b_last) + _dot(v.T.astype(BF16), k_dec)

        gh = gh_ref[:, cols]
        o = _rms(o, gon_ref[:, cols]) * (gh / (1.0 + jnp.exp(-gh)))
        o_ref[:, cols] = o.astype(BF16)


def _hg_call(zhg, lb_l, gon_l, batch, seq_len, ts):
    t = zhg.shape[0]
    nt = seq_len // ts
    hb = HG_HEADS_PER_STEP
    groups = HG_HEADS // hb
    part = lambda p: pl.BlockSpec((ts, hb * HG_DIM), lambda b, g, i: (b * nt + i, p * groups + g))
    vec = pl.BlockSpec((1, hb * HG_DIM), lambda b, g, i: (0, g))
    return pl.pallas_call(
        functools.partial(_hg_kernel, heads=hb),
        grid=(batch, groups, nt),
        in_specs=[part(0), part(1), part(2), part(3), vec, vec],
        out_specs=[pl.BlockSpec((ts, hb * HG_DIM), lambda b, g, i: (b * nt + i, g)),
                   pl.BlockSpec((1, hb, HG_DIM, HG_DIM), lambda b, g, i: (b, g, 0, 0))],
        out_shape=[jax.ShapeDtypeStruct((t, HG_WIDTH), BF16),
                   jax.ShapeDtypeStruct((batch, HG_HEADS, HG_DIM, HG_DIM), F32)],
        scratch_shapes=[pltpu.VMEM((hb, ts, HG_DIM), F32), pltpu.VMEM((hb, ts, ts), F32),
                        pltpu.VMEM((ts, ts), jnp.int32)],
        compiler_params=_params(("arbitrary", "arbitrary", "arbitrary")),
        name="hgrn2_prompt",
    )(zhg, zhg, zhg, zhg, lb_l, gon_l)


def _post_kernel(x_ref, oa_ref, oh_ref, ga1_ref, sc2_ref, sh2_ref, ga2_ref, gpm_ref, gpf_ref, gqf_ref,
                 woa_ref, woh_ref, wup_ref, wdn_ref, out_ref, *, ff_chunk):
    x = x_ref[...]
    y = _dot(oa_ref[...], woa_ref[...]) + _dot(oh_ref[...], woh_ref[...])
    x1 = x + ga1_ref[0] * _rms(y, gpm_ref[...])
    h2 = (_rms(x1, gpf_ref[...]) * (1.0 + sc2_ref[0]) + sh2_ref[0]).astype(BF16)
    m = jnp.zeros(x.shape, F32)
    for c in range(D_FF // ff_chunk):
        cols = slice(c * ff_chunk, (c + 1) * ff_chunk)
        u = jnp.square(jnp.maximum(_dot(h2, wup_ref[:, cols]), 0.0)).astype(BF16)
        m = m + _dot(u, wdn_ref[cols, :])
    out_ref[...] = x1 + ga2_ref[0] * _rms(m, gqf_ref[...])


def _post_call(x2, oa, oh, mods, gains, w, rows_per_mod, tm):
    t = x2.shape[0]
    row = lambda i: (i, 0)
    fixed = lambda i: (0, 0)
    mod_rows = mods[0].shape[1]
    mod = pl.BlockSpec((1, mod_rows, D_MODEL), lambda i: ((i * tm) // rows_per_mod, 0, 0))
    gain = pl.BlockSpec((1, D_MODEL), fixed)
    wspec = lambda a: pl.BlockSpec(a.shape, fixed)
    return pl.pallas_call(
        functools.partial(_post_kernel, ff_chunk=1024),
        grid=(t // tm,),
        in_specs=[pl.BlockSpec((tm, D_MODEL), row),
                  pl.BlockSpec((tm, ATT_WIDTH), row),
                  pl.BlockSpec((tm, HG_WIDTH), row),
                  mod, mod, mod, mod, gain, gain, gain,
                  wspec(w["woa"]), wspec(w["woh"]), wspec(w["wup"]), wspec(w["wdn"])],
        out_specs=pl.BlockSpec((tm, D_MODEL), row),
        out_shape=jax.ShapeDtypeStruct((t, D_MODEL), F32),
        compiler_params=_params(("arbitrary",)),
        name="out_proj_mlp",
    )(x2, oa, oh, *mods, *gains, w["woa"], w["woh"], w["wup"], w["wdn"])


def _dec_in_kernel(x_ref, g_ref, sc_ref, sh_ref, w_ref, z_ref):
    h = (_rms(x_ref[...], g_ref[...]) * (1.0 + sc_ref[...]) + sh_ref[...]).astype(BF16)
    z_ref[...] = _dot(h, w_ref[...])


def _dec_in_call(x2, g, sc, sh, w_in_l):
    rows = x2.shape[0]
    n_in = w_in_l.shape[1]
    tn = 512
    full = pl.BlockSpec((rows, D_MODEL), lambda j: (0, 0))
    return pl.pallas_call(
        _dec_in_kernel,
        grid=(n_in // tn,),
        in_specs=[full, pl.BlockSpec((1, D_MODEL), lambda j: (0, 0)), full, full,
                  pl.BlockSpec((D_MODEL, tn), lambda j: (0, j))],
        out_specs=pl.BlockSpec((rows, tn), lambda j: (0, j)),
        out_shape=jax.ShapeDtypeStruct((rows, n_in), F32),
        compiler_params=_params(("arbitrary",)),
        name="in_proj_sample",
    )(x2, g, sc, sh, w_in_l)


def _dec_att_kernel(pt_ref, q_ref, kn_ref, vn_ref, bias_ref, rpe_ref, *refs, pages_per_step, n_blocks):
    k_refs = refs[:pages_per_step]
    v_refs = refs[pages_per_step:2 * pages_per_step]
    o_ref = refs[2 * pages_per_step]
    m_s, l_s, g_s, acc_s = refs[2 * pages_per_step + 1:]
    j = pl.program_id(1)
    last_step = j == pl.num_programs(1) - 1
    bps = pages_per_step // 2
    q = q_ref[0] * Q_SCALE
    c_far = rpe_ref[:, :, RPE_BUCKETS - 1:RPE_BUCKETS] * LOG2E
    stat = (1, ATT_HEADS, 1, LANES)

    scores = lambda k_page: jnp.sum(k_page * q, axis=1, keepdims=True)
    lane_sum = lambda x: jnp.sum(x, axis=-1, keepdims=True)
    lane_max = lambda x: jnp.max(x, axis=-1, keepdims=True)

    for blk in range(bps):
        s0 = scores(k_refs[2 * blk][0, 0])
        s1 = scores(k_refs[2 * blk + 1][0, 0])
        g = lane_sum(s0 + s1)
        s0 = s0 + c_far
        if blk == bps - 1:
            s1 = s1 + jnp.where(last_step, bias_ref[...], c_far)
        else:
            s1 = s1 + c_far
        m = jnp.maximum(lane_max(s0), lane_max(s1))
        p0 = jnp.exp2(s0 - m)
        p1 = jnp.exp2(s1 - m)
        row = j * bps + blk
        m_s[pl.ds(row, 1)] = jnp.broadcast_to(m, stat)
        l_s[pl.ds(row, 1)] = jnp.broadcast_to(lane_sum(p0 + p1), stat)
        g_s[pl.ds(row, 1)] = jnp.broadcast_to(g, stat)
        acc_s[pl.ds(row, 1)] = (p0 * v_refs[2 * blk][0, 0] + p1 * v_refs[2 * blk + 1][0, 0])[None]

    @pl.when(last_step)
    def _():
        s_new = scores(kn_ref[0]) + rpe_ref[:, :, 0:1] * LOG2E
        lane = lax.broadcasted_iota(jnp.int32, (ATT_HEADS, ATT_HEAD_DIM, LANES), 2)
        m_s[n_blocks:n_blocks + 1] = s_new[None]
        l_s[n_blocks:n_blocks + 1] = jnp.ones(stat, F32)
        g_s[n_blocks:n_blocks + 1] = jnp.zeros(stat, F32)
        acc_s[n_blocks:n_blocks + 1] = jnp.where(lane == 0, vn_ref[0], 0.0)[None]
        rows = n_blocks + 1
        n = lax.broadcasted_iota(jnp.int32, m_s.shape, 0)
        gate = jnp.where(n < n_blocks, g_s[...], -jnp.inf)
        sel = _top_blocks(gate, n, rows, lambda jj: jj < n_blocks) | (n == n_blocks)
        mm = jnp.where(sel, m_s[...], -jnp.inf)
        w = jnp.exp2(mm - jnp.max(mm, axis=0, keepdims=True))
        den = jnp.sum(w * l_s[...], axis=0)
        num = lane_sum(jnp.sum(w * acc_s[...], axis=0))
        o_ref[0] = jnp.broadcast_to(num / den[:, :, 0:1], o_ref.shape[1:]).astype(BF16)


def _dec_att_call(page_table, q4, kn4, vn4, dec_bias, rpe_heads, cache_kT, cache_vT, layer, pages_per_step):
    n_seq, n_pages = page_table.shape
    n_blocks = n_pages * PAGE_SIZE // MOBA_BLOCK
    steps = n_pages // pages_per_step
    rows = n_blocks + 1
    tile = (ATT_HEADS, ATT_HEAD_DIM, LANES)
    vec = pl.BlockSpec((1, *tile), lambda b, j, pt: (b, 0, 0, 0))

    def page_spec(p):
        return pl.BlockSpec((1, 1, ATT_HEADS, ATT_HEAD_DIM, PAGE_SIZE),
                            lambda b, j, pt: (layer, pt[b, j * pages_per_step + p], 0, 0, 0))

    grid_spec = pltpu.PrefetchScalarGridSpec(
        num_scalar_prefetch=1,
        grid=(n_seq, steps),
        in_specs=[vec, vec, vec,
                  pl.BlockSpec((ATT_HEADS, 1, PAGE_SIZE), lambda b, j, pt: (0, 0, 0)),
                  pl.BlockSpec((ATT_HEADS, 1, RPE_BUCKETS), lambda b, j, pt: (0, 0, 0))]
                 + [page_spec(p) for p in range(pages_per_step)]
                 + [page_spec(p) for p in range(pages_per_step)],
        out_specs=vec,
        scratch_shapes=[pltpu.VMEM((rows, ATT_HEADS, 1, LANES), F32),
                        pltpu.VMEM((rows, ATT_HEADS, 1, LANES), F32),
                        pltpu.VMEM((rows, ATT_HEADS, 1, LANES), F32),
                        pltpu.VMEM((rows, *tile), F32)],
    )
    return pl.pallas_call(
        functools.partial(_dec_att_kernel, pages_per_step=pages_per_step, n_blocks=n_blocks),
        grid_spec=grid_spec,
        out_shape=jax.ShapeDtypeStruct((n_seq, *tile), BF16),
        compiler_params=_params(("arbitrary", "arbitrary")),
        name="moba_attention_sample",
    )(page_table, q4, kn4, vn4, dec_bias, rpe_heads,
      *([cache_kT] * pages_per_step), *([cache_vT] * pages_per_step))


def _dec_hg_kernel(qh_ref, fh_ref, ih_ref, gh_ref, lb_ref, gon_ref, s_ref, o_ref, so_ref):
    r = lax.broadcasted_iota(jnp.int32, (HG_DIM, HG_DIM), 0)
    c = lax.broadcasted_iota(jnp.int32, (HG_DIM, HG_DIM), 1)
    eye = r == c

    def column(x):
        return jnp.sum(jnp.where(eye, jnp.broadcast_to(x, (HG_DIM, HG_DIM)), 0.0), axis=1, keepdims=True)

    for h in range(HG_HEADS):
        cols = slice(h * HG_DIM, (h + 1) * HG_DIM)
        lb = lb_ref[:, cols]
        fpre = fh_ref[0][:, cols]
        qh = qh_ref[0][:, cols]
        gh = gh_ref[0][:, cols]
        v = ih_ref[0][:, cols]
        f = jnp.exp(_log_forget(fpre, lb))
        kk = (1.0 - lb) / (1.0 + jnp.exp(fpre))
        q = qh / (1.0 + jnp.exp(-qh))
        s_new = column(f) * s_ref[0, h] + column(kk) * v
        so_ref[0, h] = s_new
        o = jnp.sum(column(q) * s_new, axis=0, keepdims=True)
        o = _rms(o, gon_ref[:, cols]) * (gh / (1.0 + jnp.exp(-gh)))
        o_ref[0, :, cols] = o.astype(BF16)


def _dec_hg_call(qh3, fh3, ih3, gh3, lb_l, gon_l, state_l):
    n_seq = qh3.shape[0]
    vec = pl.BlockSpec((1, 1, HG_WIDTH), lambda b: (b, 0, 0))
    par = pl.BlockSpec((1, HG_WIDTH), lambda b: (0, 0))
    st = pl.BlockSpec((1, HG_HEADS, HG_DIM, HG_DIM), lambda b: (b, 0, 0, 0))
    return pl.pallas_call(
        _dec_hg_kernel,
        grid=(n_seq,),
        in_specs=[vec, vec, vec, vec, par, par, st],
        out_specs=[vec, st],
        out_shape=[jax.ShapeDtypeStruct((n_seq, 1, HG_WIDTH), BF16),
                   jax.ShapeDtypeStruct(state_l.shape, F32)],
        compiler_params=_params(("arbitrary",)),
        name="hgrn2_sample",
    )(qh3, fh3, ih3, gh3, lb_l, gon_l, state_l)


def _layer_weights(w_in, w_out, w_up, w_down, l):
    a = ATT_WIDTH
    wi = w_in[l]
    wq, wk, wv, whg = wi[:, :a], wi[:, a:2 * a], wi[:, 2 * a:3 * a], wi[:, 3 * a:]
    wkp = jnp.pad(wk.reshape(D_MODEL, ATT_HEADS, ATT_HEAD_DIM),
                  ((0, 0), (0, 0), (0, HEAD_PAD - ATT_HEAD_DIM))).reshape(D_MODEL, ATT_HEADS * HEAD_PAD)
    return {"wqT": wq.T, "wkT": wk.T, "wvT": wv.T, "wkp": wkp, "whg": whg, "w_in": wi,
            "woa": w_out[l, :a], "woh": w_out[l, a:], "wup": w_up[l], "wdn": w_down[l]}


def kernel(x_prompt, x_sample, c_prompt, c_sample, cache_k, cache_v, state_hgrn, page_table, w_ada, b_ada,
           g_pre_mix, g_post_mix, g_pre_ffn, g_post_ffn, w_in, lb_param, g_onorm, w_out, w_up, w_down,
           rpe_table):
    batch, seq_len, _ = x_prompt.shape
    n_seq, dec_seq, _ = x_sample.shape
    n_pages = page_table.shape[1]
    assert dec_seq == 1 and seq_len % MOBA_BLOCK == 0 and (n_pages * PAGE_SIZE) % MOBA_BLOCK == 0
    assert seq_len // MOBA_BLOCK <= GATE_SLOTS
    tm = 512 if seq_len % 512 == 0 else MOBA_BLOCK
    pages_per_step = next(p for p in (32, 16, 8, 4, 2) if n_pages % p == 0)
    a = ATT_WIDTH
    hd = (ATT_HEADS, ATT_HEAD_DIM)

    w_in_b, w_out_b = w_in.astype(BF16), w_out.astype(BF16)
    w_up_b, w_down_b = w_up.astype(BF16), w_down.astype(BF16)

    n_c = batch + n_seq
    c_rows = ((n_c + 7) // 8) * 8
    c_all = jnp.pad(jnp.concatenate([c_prompt, c_sample], axis=0), ((0, c_rows - n_c), (0, 0)))
    mod = _ada_call(c_all, w_ada, b_ada)
    lb = _lb_call(lb_param)
    rpe_flat = rpe_table.T.reshape(-1)
    rpe_heads = rpe_table.T.reshape(ATT_HEADS, 1, RPE_BUCKETS)
    bias_tiles = _bias_tile_call(rpe_flat)
    dec_bias = _dec_bias_call(rpe_heads)
    cache_kT = jnp.transpose(cache_k, (0, 1, 3, 4, 2))
    cache_vT = jnp.transpose(cache_v, (0, 1, 3, 4, 2))

    xp = x_prompt.reshape(batch * seq_len, D_MODEL)
    xs = x_sample.reshape(n_seq, D_MODEL)
    k_p, v_p, s_p, k_s, v_s, s_s = [], [], [], [], [], []
    for l in range(DEPTH):
        w = _layer_weights(w_in_b, w_out_b, w_up_b, w_down_b, l)
        row = lambda arr: arr[l].reshape(1, -1)
        gains = (row(g_post_mix), row(g_pre_ffn), row(g_post_ffn))
        lb_l, gon_l = row(lb), row(g_onorm)

        mp = mod[l, :, :batch].reshape(6, batch, 1, D_MODEL)
        qT, kT, vT, vTb, kaug, kmean, zhg = _in_proj_call(xp, row(g_pre_mix), mp[1], mp[0], w, batch,
                                                          seq_len, tm)
        qaug = _gate_call(rpe_flat, kmean.reshape(batch, seq_len // MOBA_BLOCK, -1), qT, batch, seq_len)
        oT = _att_call(qaug, kaug, vTb, bias_tiles.reshape(ATT_HEADS, 2 * MOBA_BLOCK, MOBA_BLOCK), batch, seq_len)
        o_att = jnp.swapaxes(oT, 1, 2).reshape(batch * seq_len, a)
        o_hg, sT = _hg_call(zhg, lb_l, gon_l, batch, seq_len, MOBA_BLOCK)
        xp = _post_call(xp, o_att, o_hg, (mp[2], mp[4], mp[3], mp[5]), gains, w, seq_len, tm)
        k_p.append(kT)
        v_p.append(vT)
        s_p.append(jnp.swapaxes(sT, -1, -2))

        ms = mod[l, :, batch:n_c]
        z = _dec_in_call(xs, row(g_pre_mix), ms[1], ms[0], w["w_in"])
        part = lambda i: z[:, i * a:(i + 1) * a]
        wide = lambda i: jnp.broadcast_to(part(i).reshape(n_seq, *hd, 1), (n_seq, *hd, LANES))
        o4 = _dec_att_call(page_table, wide(0), wide(1), wide(2), dec_bias, rpe_heads, cache_kT, cache_vT,
                           l, pages_per_step)
        o_att_s = o4[..., 0].reshape(n_seq, a)
        row3 = lambda i: part(i).reshape(n_seq, 1, a)
        o_hgs, s_new = _dec_hg_call(row3(3), row3(4), row3(5), row3(6), lb_l, gon_l, state_hgrn[l])
        ms4 = ms.reshape(6, 1, n_seq, D_MODEL)
        xs = _post_call(xs, o_att_s, o_hgs.reshape(n_seq, a), (ms4[2], ms4[4], ms4[3], ms4[5]), gains, w,
                        n_seq, n_seq)
        k_s.append(part(1))
        v_s.append(part(2))
        s_s.append(s_new)

    rows_out = lambda parts: jnp.transpose(jnp.stack(parts).reshape(DEPTH, batch, *hd, seq_len),
                                           (0, 1, 4, 2, 3))
    return (xp.reshape(batch, seq_len, D_MODEL),
            xs.reshape(n_seq, 1, D_MODEL),
            rows_out(k_p),
            rows_out(v_p),
            jnp.stack(s_p),
            jnp.stack(k_s).reshape(DEPTH, n_seq, 1, *hd),
            jnp.stack(v_s).reshape(DEPTH, n_seq, 1, *hd),
            jnp.stack(s_s))
```

```python
import functools
import math

import numpy as np
import jax
import jax.numpy as jnp
from jax import lax
from jax.experimental import pallas as pl
from jax.experimental.pallas import tpu as pltpu

F32 = jnp.float32
BF16 = jnp.bfloat16

D_MODEL = 1024
DEPTH = 4
ATT_HEADS = 8
ATT_HEAD_DIM = 64
ATT_WIDTH = ATT_HEADS * ATT_HEAD_DIM
MOBA_BLOCK = 256
MOBA_TOPK = 3
PAGE_SIZE = 128
RPE_BUCKETS = 32
RPE_MAX_DIST = 128
HG_HEADS = 4
HG_DIM = 128
HG_WIDTH = HG_HEADS * HG_DIM
D_FF = 4 * D_MODEL
EPS = 1e-6

LANES = 128
HEAD_PAD = 128
GATE_SLOTS = 32
ATT_HEADS_PER_STEP = 4
ATT_SPAN = 4
ATT_DEN_ROWS = 16
HG_HEADS_PER_STEP = 4
LOG2E = 1.4426950408889634
Q_SCALE = ATT_HEAD_DIM ** -0.5 * LOG2E
NEG = -30000.0
VMEM_LIMIT = 56 * 1024 * 1024

NT = (((1,), (1,)), ((), ()))


def _bucket_starts():
    max_exact = RPE_BUCKETS // 2
    d = np.arange(0, RPE_MAX_DIST + 1)
    dd = np.maximum(d, max_exact).astype(np.float32)
    large = max_exact + (np.log(dd / np.float32(max_exact)) / np.float32(math.log(RPE_MAX_DIST / max_exact))
                         * np.float32(RPE_BUCKETS - max_exact)).astype(np.int32)
    large = np.minimum(large, RPE_BUCKETS - 1)
    b = np.where(d < max_exact, d, large)
    return tuple(int(np.argmax(b >= k)) for k in range(RPE_BUCKETS))


BUCKET_STARTS = _bucket_starts()


def _params(sem):
    return pltpu.CompilerParams(dimension_semantics=sem, vmem_limit_bytes=VMEM_LIMIT)


def _rms(x, g):
    return x * lax.rsqrt(jnp.mean(x * x, axis=-1, keepdims=True) + EPS) * g


def _dot(a, b):
    return jnp.dot(a, b, preferred_element_type=F32)


def _dot_exact_lhs(a, x):
    hi = x.astype(BF16)
    r = x - hi.astype(F32)
    mid = r.astype(BF16)
    lo = (r - mid.astype(F32)).astype(BF16)
    return _dot(a, hi) + _dot(a, mid) + _dot(a, lo)


def _top_blocks(gate, n, sentinel, hit_ok):
    sel = jnp.zeros(gate.shape, jnp.bool_)
    for j in range(MOBA_TOPK):
        mx = jnp.max(gate, axis=0, keepdims=True)
        idx = jnp.min(jnp.where(gate == mx, n, sentinel), axis=0, keepdims=True)
        hit = n == idx
        sel = sel | (hit & hit_ok(j))
        gate = jnp.where(hit, -jnp.inf, gate)
    return sel


def _layer_spec(arr, layer):
    return pl.BlockSpec((None, *arr.shape[1:]), lambda *_: (layer, 0, 0))


def _mod_spec(mods, layer, j, rows_per_mod, tm):
    return pl.BlockSpec((None, None, None, *mods.shape[3:]),
                        lambda i: (layer, j, (i * tm) // rows_per_mod, 0, 0))


def _ada_kernel(c_ref, w_ref, b_ref, o_ref):
    c = c_ref[...]
    act = (c / (1.0 + jnp.exp(-c))).astype(BF16)
    o_ref[0, 0] = _dot(act, w_ref[0].astype(BF16)) + b_ref[0, 0]


def _ada_call(c_all, w_ada, b_ada):
    rows = c_all.shape[0]
    return pl.pallas_call(
        _ada_kernel,
        grid=(DEPTH, 6),
        in_specs=[pl.BlockSpec((rows, D_MODEL), lambda l, j: (0, 0)),
                  pl.BlockSpec((1, D_MODEL, D_MODEL), lambda l, j: (l, 0, j)),
                  pl.BlockSpec((1, 1, 1, D_MODEL), lambda l, j: (l, j, 0, 0))],
        out_specs=pl.BlockSpec((1, 1, rows, D_MODEL), lambda l, j: (l, j, 0, 0)),
        out_shape=jax.ShapeDtypeStruct((DEPTH, 6, rows, D_MODEL), F32),
        compiler_params=_params(("arbitrary", "arbitrary")),
        name="ada_mod",
    )(c_all, w_ada, b_ada.reshape(DEPTH, 6, 1, D_MODEL))


def _lb_kernel(p_ref, o_ref):
    p = p_ref[...]
    e = jnp.exp(p - jnp.max(p, axis=0, keepdims=True))
    sm = e / jnp.sum(e, axis=0, keepdims=True)
    acc = jnp.zeros((1, HG_WIDTH), F32)
    for l in range(DEPTH):
        o_ref[l:l + 1, :] = acc
        if l + 1 < DEPTH:
            acc = acc + sm[l + 1:l + 2, :]


def _lb_call(lb_param):
    return pl.pallas_call(
        _lb_kernel,
        out_shape=jax.ShapeDtypeStruct((DEPTH, HG_WIDTH), F32),
        name="hgrn_lower_bounds",
    )(lb_param)


def _bias_lookup(d, table):
    val = table(RPE_BUCKETS - 1)
    for b in range(RPE_BUCKETS - 2, -1, -1):
        val = jnp.where(d < BUCKET_STARTS[b + 1], table(b), val)
    return val * LOG2E


def _bias_tile_kernel(rpe_ref, o_ref):
    h = pl.program_id(0)
    kk = lax.broadcasted_iota(jnp.int32, (MOBA_BLOCK, MOBA_BLOCK), 0)
    qq = lax.broadcasted_iota(jnp.int32, (MOBA_BLOCK, MOBA_BLOCK), 1)
    d = qq - kk
    table = lambda b: rpe_ref[h * RPE_BUCKETS + b]
    o_ref[0, 0] = _bias_lookup(d + MOBA_BLOCK, table)
    o_ref[0, 1] = jnp.where(d >= 0, _bias_lookup(d, table), NEG)


def _bias_tile_call(rpe_flat):
    return pl.pallas_call(
        _bias_tile_kernel,
        grid=(ATT_HEADS,),
        in_specs=[pl.BlockSpec(memory_space=pltpu.SMEM)],
        out_specs=pl.BlockSpec((1, 2, MOBA_BLOCK, MOBA_BLOCK), lambda h: (h, 0, 0, 0)),
        out_shape=jax.ShapeDtypeStruct((ATT_HEADS, 2, MOBA_BLOCK, MOBA_BLOCK), F32),
        compiler_params=_params(("arbitrary",)),
        name="rpe_bias_tiles",
    )(rpe_flat)


def _dec_bias_kernel(rpe_ref, o_ref):
    assert BUCKET_STARTS[-1] <= PAGE_SIZE + 1
    d = PAGE_SIZE - lax.broadcasted_iota(jnp.int32, (ATT_HEADS, 1, PAGE_SIZE), 2)
    o_ref[...] = _bias_lookup(d, lambda b: rpe_ref[:, :, b:b + 1])


def _dec_bias_call(rpe_heads):
    return pl.pallas_call(
        _dec_bias_kernel,
        out_shape=jax.ShapeDtypeStruct((ATT_HEADS, 1, PAGE_SIZE), F32),
        name="rpe_bias_decode",
    )(rpe_heads)


def _in_proj_kernel(x_ref, g_ref, sc_ref, sh_ref, wqT_ref, wkT_ref, wvT_ref, wkp_ref, whg_ref,
                    qT_ref, kT_ref, vT_ref, vTb_ref, kaug_ref, kmean_ref, zhg_ref, *, blocks_per_seq):
    x = x_ref[...]
    tm = x.shape[0]
    h = (_rms(x, g_ref[...]) * (1.0 + sc_ref[...]) + sh_ref[...]).astype(BF16)
    transposed = lambda w_ref: lax.dot_general(w_ref[...], h, NT, preferred_element_type=F32)
    qT_ref[0] = (transposed(wqT_ref) * Q_SCALE).astype(BF16)
    kT_ref[0] = transposed(wkT_ref)
    vT = transposed(wvT_ref)
    vT_ref[0] = vT
    vTb_ref[0] = vT.astype(BF16)
    zhg_ref[...] = _dot(h, whg_ref[...])
    kp = _dot(h, wkp_ref[...])
    nb = tm // MOBA_BLOCK
    for r in range(nb):
        kmean_ref[r] = jnp.mean(kp[r * MOBA_BLOCK:(r + 1) * MOBA_BLOCK], axis=0, keepdims=True)
    row = lax.broadcasted_iota(jnp.int32, kp.shape, 0)
    lane = lax.broadcasted_iota(jnp.int32, kp.shape, 1) % HEAD_PAD
    blk = (pl.program_id(0) * nb + row // MOBA_BLOCK) % blocks_per_seq
    onehot = (lane == blk + ATT_HEAD_DIM) | (lane == blk + ATT_HEAD_DIM + GATE_SLOTS)
    kaug_ref[...] = jnp.where(onehot, 1.0, kp).astype(BF16)


def _in_proj_call(x2, g, mods, w, layer, batch, seq_len, tm):
    t = x2.shape[0]
    tps = seq_len // tm
    nbt = tm // MOBA_BLOCK
    hp = ATT_HEADS * HEAD_PAD
    row = lambda i: (i, 0)
    tr = pl.BlockSpec((1, ATT_WIDTH, tm), lambda i: (i // tps, 0, i % tps))
    wspec = lambda a: _layer_spec(a, layer)
    tshape = lambda dt: jax.ShapeDtypeStruct((batch, ATT_WIDTH, seq_len), dt)
    return pl.pallas_call(
        functools.partial(_in_proj_kernel, blocks_per_seq=seq_len // MOBA_BLOCK),
        grid=(t // tm,),
        in_specs=[pl.BlockSpec((tm, D_MODEL), row),
                  _layer_spec(g, layer),
                  _mod_spec(mods, layer, 1, seq_len, tm),
                  _mod_spec(mods, layer, 0, seq_len, tm),
                  wspec(w["wqT"]), wspec(w["wkT"]), wspec(w["wvT"]), wspec(w["wkp"]), wspec(w["whg"])],
        out_specs=[tr, tr, tr, tr,
                   pl.BlockSpec((tm, hp), row),
                   pl.BlockSpec((nbt, 1, hp), lambda i: (i, 0, 0)),
                   pl.BlockSpec((tm, 4 * HG_WIDTH), row)],
        out_shape=[tshape(BF16), tshape(F32), tshape(F32), tshape(BF16),
                   jax.ShapeDtypeStruct((t, hp), BF16),
                   jax.ShapeDtypeStruct((t // MOBA_BLOCK, 1, hp), F32),
                   jax.ShapeDtypeStruct((t, 4 * HG_WIDTH), F32)],
        compiler_params=_params(("arbitrary",)),
        name="in_proj",
    )(x2, g, mods, mods, w["wqT"], w["wkT"], w["wvT"], w["wkp"], w["whg"])


def _gated_queries(q, kmean, c_far, own, q_scr):
    nbs = kmean.shape[0]
    tq = q.shape[1]
    n = lax.broadcasted_iota(jnp.int32, (nbs, tq), 0)
    far = n <= own - 2
    gate = jnp.where(n < own, _dot(kmean.astype(BF16), q), -jnp.inf)
    sel = _top_blocks(gate, n, nbs, lambda j: j < own)
    c = jnp.full((nbs, tq), c_far, F32)
    c_hi = c.astype(BF16).astype(F32)
    c_lo = c - c_hi
    p_hi = jnp.where(sel, jnp.where(far, c_hi, 0.0), NEG)
    p_hi = jnp.where(n >= own, 0.0, p_hi)
    p_lo = jnp.where(sel & far, c_lo, 0.0)
    q_scr[0:ATT_HEAD_DIM, :] = q
    q_scr[ATT_HEAD_DIM:HEAD_PAD, :] = jnp.zeros((HEAD_PAD - ATT_HEAD_DIM, tq), BF16)
    q_scr[ATT_HEAD_DIM:ATT_HEAD_DIM + nbs, :] = p_hi.astype(BF16)
    q_scr[ATT_HEAD_DIM + GATE_SLOTS:ATT_HEAD_DIM + GATE_SLOTS + nbs, :] = p_lo.astype(BF16)


def _att_kernel(rpe_ref, qT_ref, kmean_ref, k_ref, vT_ref, bias_ref, o_ref, q_scr, s_a, s_b, x_a, x_b, p_scr,
                m_scr, a_scr, acc_scr, *, heads, n_key_blocks):
    qi = pl.program_id(2)
    for h in range(heads):
        head = pl.program_id(1) * heads + h
        _gated_queries(qT_ref[0, h * ATT_HEAD_DIM:(h + 1) * ATT_HEAD_DIM, :],
                       kmean_ref[0][:, h * HEAD_PAD:h * HEAD_PAD + ATT_HEAD_DIM],
                       rpe_ref[head * RPE_BUCKETS + RPE_BUCKETS - 1] * LOG2E, qi, q_scr.at[h])
    ones = jnp.ones((ATT_DEN_ROWS, ATT_SPAN * MOBA_BLOCK), BF16)

    def scores(first_block, n_blocks, near, s_scr, x_scr):
        keys = n_blocks * MOBA_BLOCK
        start = pl.multiple_of(first_block * MOBA_BLOCK, MOBA_BLOCK)
        for h in range(heads):
            s = _dot(k_ref[pl.ds(start, keys), h * HEAD_PAD:(h + 1) * HEAD_PAD], q_scr[h])
            if near:
                s = s + bias_ref[h, (2 - n_blocks) * MOBA_BLOCK:2 * MOBA_BLOCK, :]
            s_scr[h, 0:keys, :] = s
            x_scr[h] = jnp.max(s, axis=0, keepdims=True)

    def absorb(first_block, n_blocks, s_scr, x_scr):
        keys = n_blocks * MOBA_BLOCK
        start = pl.multiple_of(first_block * MOBA_BLOCK, MOBA_BLOCK)
        for h in range(heads):
            m = m_scr[h]
            m_new = jnp.maximum(m, x_scr[h])
            a_scr[h] = jnp.exp2(m - m_new)
            m_scr[h] = m_new
            p_scr[h, 0:keys, :] = jnp.exp2(s_scr[h, 0:keys, :] - m_new).astype(BF16)
        for h in range(heads):
            v = vT_ref[0, h * ATT_HEAD_DIM:(h + 1) * ATT_HEAD_DIM, pl.ds(start, keys)]
            v1 = jnp.concatenate([v, ones[:, 0:keys]], axis=0)
            acc_scr[h] = a_scr[h] * acc_scr[h] + _dot(v1, p_scr[h, 0:keys, :])

    def span(first_block, n_blocks, near):
        scores(first_block, n_blocks, near, s_a, x_a)
        absorb(first_block, n_blocks, s_a, x_a)

    m_scr[...] = jnp.full(m_scr.shape, -jnp.inf, F32)
    acc_scr[...] = jnp.zeros(acc_scr.shape, F32)

    @pl.when(qi == 0)
    def _():
        span(0, 1, True)

    @pl.when(qi >= 1)
    def _():
        span(qi - 1, 2, True)

    n_far = jnp.maximum(qi - 1, 0)
    n_full = n_far // ATT_SPAN
    ahead = lambda i: jnp.minimum(i * ATT_SPAN, n_key_blocks - ATT_SPAN)

    @pl.when(n_full > 0)
    def _():
        scores(0, ATT_SPAN, False, s_a, x_a)

    @pl.loop(0, n_full // 2)
    def _(j):
        scores(ahead(2 * j + 1), ATT_SPAN, False, s_b, x_b)
        absorb(2 * j * ATT_SPAN, ATT_SPAN, s_a, x_a)
        scores(ahead(2 * j + 2), ATT_SPAN, False, s_a, x_a)
        absorb((2 * j + 1) * ATT_SPAN, ATT_SPAN, s_b, x_b)

    @pl.when(n_full % 2 == 1)
    def _():
        absorb((n_full - 1) * ATT_SPAN, ATT_SPAN, s_a, x_a)

    done = n_full * ATT_SPAN
    size = ATT_SPAN // 2
    while size >= 1:
        @pl.when((n_far - done) & size != 0)
        def _(done=done, size=size):
            span(done, size, False)
        done = done + ((n_far - done) & size)
        size //= 2

    for h in range(heads):
        acc = acc_scr[h]
        den = acc[ATT_HEAD_DIM:ATT_HEAD_DIM + 1, :]
        o_ref[0, h * ATT_HEAD_DIM:(h + 1) * ATT_HEAD_DIM, :] = (acc[0:ATT_HEAD_DIM, :] / den).astype(BF16)


def _att_call(rpe_flat, qT, kmean, kaug, vT, bias_tiles, batch, seq_len):
    nq = seq_len // MOBA_BLOCK
    hb = ATT_HEADS_PER_STEP
    return pl.pallas_call(
        functools.partial(_att_kernel, heads=hb, n_key_blocks=nq),
        grid=(batch, ATT_HEADS // hb, nq),
        in_specs=[pl.BlockSpec(memory_space=pltpu.SMEM),
                  pl.BlockSpec((1, hb * ATT_HEAD_DIM, MOBA_BLOCK), lambda b, h, i: (b, h, i)),
                  pl.BlockSpec((1, nq, hb * HEAD_PAD), lambda b, h, i: (b, 0, h)),
                  pl.BlockSpec((seq_len, hb * HEAD_PAD), lambda b, h, i: (b, h)),
                  pl.BlockSpec((1, hb * ATT_HEAD_DIM, seq_len), lambda b, h, i: (b, h, 0)),
                  pl.BlockSpec((hb, 2 * MOBA_BLOCK, MOBA_BLOCK), lambda b, h, i: (h, 0, 0))],
        out_specs=pl.BlockSpec((1, hb * ATT_HEAD_DIM, MOBA_BLOCK), lambda b, h, i: (b, h, i)),
        out_shape=jax.ShapeDtypeStruct((batch, ATT_WIDTH, seq_len), BF16),
        scratch_shapes=[pltpu.VMEM((hb, HEAD_PAD, MOBA_BLOCK), BF16),
                        pltpu.VMEM((hb, ATT_SPAN * MOBA_BLOCK, MOBA_BLOCK), F32),
                        pltpu.VMEM((hb, ATT_SPAN * MOBA_BLOCK, MOBA_BLOCK), F32),
                        pltpu.VMEM((hb, 1, MOBA_BLOCK), F32),
                        pltpu.VMEM((hb, 1, MOBA_BLOCK), F32),
                        pltpu.VMEM((hb, ATT_SPAN * MOBA_BLOCK, MOBA_BLOCK), BF16),
                        pltpu.VMEM((hb, 1, MOBA_BLOCK), F32),
                        pltpu.VMEM((hb, 1, MOBA_BLOCK), F32),
                        pltpu.VMEM((hb, ATT_HEAD_DIM + ATT_DEN_ROWS, MOBA_BLOCK), F32)],
        compiler_params=_params(("arbitrary", "arbitrary", "arbitrary")),
        name="moba_attention",
    )(rpe_flat, qT, kmean, kaug, vT, bias_tiles)


def _log_forget(fpre, lb):
    log_sig = jnp.minimum(fpre, 0.0) - jnp.log(1.0 + jnp.exp(-jnp.abs(fpre)))
    a1 = jnp.log(lb)
    a2 = jnp.log1p(-lb) + log_sig
    return jnp.maximum(a1, a2) + jnp.log(1.0 + jnp.exp(-jnp.abs(a1 - a2)))


def _mid_rows(b_scr, b, row, m, ts):
    if m == 1:
        return jnp.where((row & 1) != 0, pltpu.roll(b, 1, 0), b)
    bcast = lambda r, n: jnp.broadcast_to(b_scr[r:r + 1, :], (n, HG_DIM))
    if m == 2:
        lo = jnp.concatenate([bcast(8 * j + 1, 8) for j in range(ts // 8)], axis=0)
        hi = jnp.concatenate([bcast(8 * j + 5, 8) for j in range(ts // 8)], axis=0)
        return jnp.where((row & 4) == 0, lo, hi)
    return jnp.concatenate([bcast(2 * m * j + m - 1, 2 * m) for j in range(ts // (2 * m))], axis=0)


def _halves(q, kk, row, m, ts):
    if m < 8:
        return jnp.where((row & m) != 0, q, kk)
    pieces = [(q if (j & 1) else kk)[j * m:(j + 1) * m] for j in range(ts // m)]
    return jnp.concatenate(pieces, axis=0)


def _hg_kernel(qh_ref, fh_ref, ih_ref, gh_ref, lb_ref, gon_ref, o_ref, sT_ref, b_scr, a_scr, lvl_scr, *, heads):
    ts = qh_ref.shape[0]
    n_levels = ts.bit_length() - 1

    @pl.when((pl.program_id(0) == 0) & (pl.program_id(1) == 0) & (pl.program_id(2) == 0))
    def _():
        t = lax.broadcasted_iota(jnp.int32, (ts, ts), 0)
        s = lax.broadcasted_iota(jnp.int32, (ts, ts), 1)
        x = t ^ s
        lvl = jnp.zeros((ts, ts), jnp.int32)
        for i in range(1, n_levels):
            lvl = lvl + (x >= (1 << i)).astype(jnp.int32)
        lvl_scr[...] = jnp.where(s < t, lvl, -1)

    @pl.when(pl.program_id(2) == 0)
    def _():
        sT_ref[...] = jnp.zeros_like(sT_ref)

    r = lax.broadcasted_iota(jnp.int32, (ts, ts), 0)
    c = lax.broadcasted_iota(jnp.int32, (ts, ts), 1)
    tri = jnp.where(c <= r, 1.0, 0.0).astype(BF16)
    row = lax.broadcasted_iota(jnp.int32, (ts, HG_DIM), 0)

    for hh in range(heads):
        cols = slice(hh * HG_DIM, (hh + 1) * HG_DIM)
        b_ref, a_ref = b_scr.at[hh], a_scr.at[hh]
        lb = lb_ref[:, cols]
        fpre = fh_ref[:, cols]
        qh = qh_ref[:, cols]
        v = ih_ref[:, cols]
        logf = _log_forget(fpre, lb)
        kk = (1.0 - lb) / (1.0 + jnp.exp(fpre))
        q = qh / (1.0 + jnp.exp(-qh))
        b = _dot_exact_lhs(tri, logf) * LOG2E
        b_ref[...] = b
        b_last = b_ref[ts - 1:ts, :]

        a_ref[...] = jnp.zeros((ts, ts), F32)
        for i in range(n_levels):
            m = 1 << i
            e = jnp.exp2(-jnp.abs(b - _mid_rows(b_ref, b, row, m, ts)))
            x = (_halves(q, kk, row, m, ts) * e).astype(BF16)
            p = lax.dot_general(x, x, NT, preferred_element_type=F32)
            if m < 8:
                owned = [(slice(t0, t0 + LANES), slice(t0, t0 + LANES)) for t0 in range(0, ts, LANES)]
            else:
                owned = [(slice(2 * m * j + m, 2 * m * (j + 1)),
                          slice(2 * m * j // LANES * LANES, (2 * m * j // LANES + 1) * LANES))
                         for j in range(ts // (2 * m))]
            for rws, cls in owned:
                a_ref[rws, cls] = jnp.where(lvl_scr[rws, cls] == i, p[rws, cls], a_ref[rws, cls])
        o = _dot(a_ref[...].astype(BF16), v.astype(BF16)) + jnp.sum(q * kk, axis=-1, keepdims=True) * v

        sT = sT_ref[0, hh]
        o = o + lax.dot_general((q * jnp.exp2(b)).astype(BF16), sT.astype(BF16), NT,
                                preferred_element_type=F32)
        k_dec = (kk * jnp.exp2(b_last - b)).astype(BF16)
        sT_ref[0, hh] = sT * jnp.exp2(b_last) + _dot(v.T.astype(BF16), k_dec)

        gh = gh_ref[:, cols]
        o = _rms(o, gon_ref[:, cols]) * (gh / (1.0 + jnp.exp(-gh)))
        o_ref[:, cols] = o.astype(BF16)


def _hg_call(zhg, lb, gon, layer, batch, seq_len, ts):
    t = zhg.shape[0]
    nt = seq_len // ts
    hb = HG_HEADS_PER_STEP
    groups = HG_HEADS // hb
    part = lambda p: pl.BlockSpec((ts, hb * HG_DIM), lambda b, g, i: (b * nt + i, p * groups + g))
    vec = pl.BlockSpec((None, 1, hb * HG_DIM), lambda b, g, i: (layer, 0, g))
    return pl.pallas_call(
        functools.partial(_hg_kernel, heads=hb),
        grid=(batch, groups, nt),
        in_specs=[part(0), part(1), part(2), part(3), vec, vec],
        out_specs=[pl.BlockSpec((ts, hb * HG_DIM), lambda b, g, i: (b * nt + i, g)),
                   pl.BlockSpec((1, hb, HG_DIM, HG_DIM), lambda b, g, i: (b, g, 0, 0))],
        out_shape=[jax.ShapeDtypeStruct((t, HG_WIDTH), BF16),
                   jax.ShapeDtypeStruct((batch, HG_HEADS, HG_DIM, HG_DIM), F32)],
        scratch_shapes=[pltpu.VMEM((hb, ts, HG_DIM), F32), pltpu.VMEM((hb, ts, ts), F32),
                        pltpu.VMEM((ts, ts), jnp.int32)],
        compiler_params=_params(("arbitrary", "arbitrary", "arbitrary")),
        name="hgrn2_prompt",
    )(zhg, zhg, zhg, zhg, lb, gon)


def _post_kernel(x_ref, oa_ref, oh_ref, ga1_ref, sc2_ref, sh2_ref, ga2_ref, gpm_ref, gpf_ref, gqf_ref,
                 woa_ref, woh_ref, wup_ref, wdn_ref, out_ref, *, ff_chunk):
    x = x_ref[...]
    y = _dot(oa_ref[...], woa_ref[...]) + _dot(oh_ref[...], woh_ref[...])
    x1 = x + ga1_ref[...] * _rms(y, gpm_ref[...])
    h2 = (_rms(x1, gpf_ref[...]) * (1.0 + sc2_ref[...]) + sh2_ref[...]).astype(BF16)
    m = jnp.zeros(x.shape, F32)
    for c in range(D_FF // ff_chunk):
        cols = slice(c * ff_chunk, (c + 1) * ff_chunk)
        u = jnp.square(jnp.maximum(_dot(h2, wup_ref[:, cols]), 0.0)).astype(BF16)
        m = m + _dot(u, wdn_ref[cols, :])
    out_ref[...] = x1 + ga2_ref[...] * _rms(m, gqf_ref[...])


def _post_call(x2, oa, oh, mods, gains, w, layer, rows_per_mod, tm):
    t = x2.shape[0]
    row = lambda i: (i, 0)
    mod = lambda j: _mod_spec(mods, layer, j, rows_per_mod, tm)
    half = lambda part: pl.BlockSpec((None, ATT_WIDTH, D_MODEL), lambda i: (layer, part, 0))
    return pl.pallas_call(
        functools.partial(_post_kernel, ff_chunk=1024),
        grid=(t // tm,),
        in_specs=[pl.BlockSpec((tm, D_MODEL), row),
                  pl.BlockSpec((tm, ATT_WIDTH), row),
                  pl.BlockSpec((tm, HG_WIDTH), row),
                  mod(2), mod(4), mod(3), mod(5)] + [_layer_spec(g, layer) for g in gains]
                 + [half(0), half(1), _layer_spec(w["wup"], layer), _layer_spec(w["wdn"], layer)],
        out_specs=pl.BlockSpec((tm, D_MODEL), row),
        out_shape=jax.ShapeDtypeStruct((t, D_MODEL), F32),
        compiler_params=_params(("arbitrary",)),
        name="out_proj_mlp",
    )(x2, oa, oh, mods, mods, mods, mods, *gains, w["wout"], w["wout"], w["wup"], w["wdn"])


def _dec_in_kernel(x_ref, g_ref, sc_ref, sh_ref, w_ref, z_ref):
    h = (_rms(x_ref[...], g_ref[...]) * (1.0 + sc_ref[...]) + sh_ref[...]).astype(BF16)
    z_ref[...] = _dot(h, w_ref[...])


def _dec_in_call(x2, g, mods, w_in, layer):
    rows = x2.shape[0]
    n_in = w_in.shape[2]
    tn = 512
    full = pl.BlockSpec((rows, D_MODEL), lambda j: (0, 0))
    mod = lambda c: pl.BlockSpec((None, None, None, rows, D_MODEL), lambda j: (layer, c, 0, 0, 0))
    return pl.pallas_call(
        _dec_in_kernel,
        grid=(n_in // tn,),
        in_specs=[full, _layer_spec(g, layer), mod(1), mod(0),
                  pl.BlockSpec((None, D_MODEL, tn), lambda j: (layer, 0, j))],
        out_specs=pl.BlockSpec((rows, tn), lambda j: (0, j)),
        out_shape=jax.ShapeDtypeStruct((rows, n_in), F32),
        compiler_params=_params(("arbitrary",)),
        name="in_proj_sample",
    )(x2, g, mods, mods, w_in)


def _dec_att_kernel(pt_ref, q_ref, kn_ref, vn_ref, bias_ref, rpe_ref, *refs, pages_per_step, n_blocks):
    k_refs = refs[:pages_per_step]
    v_refs = refs[pages_per_step:2 * pages_per_step]
    o_ref = refs[2 * pages_per_step]
    m_s, l_s, g_s, acc_s = refs[2 * pages_per_step + 1:]
    j = pl.program_id(1)
    last_step = j == pl.num_programs(1) - 1
    bps = pages_per_step // 2
    q = q_ref[0] * Q_SCALE
    c_far = rpe_ref[:, :, RPE_BUCKETS - 1:RPE_BUCKETS] * LOG2E
    stat = (1, ATT_HEADS, 1, LANES)

    scores = lambda k_page: jnp.sum(k_page * q, axis=1, keepdims=True)
    lane_sum = lambda x: jnp.sum(x, axis=-1, keepdims=True)
    lane_max = lambda x: jnp.max(x, axis=-1, keepdims=True)

    for blk in range(bps):
        s0 = scores(k_refs[2 * blk][0, 0])
        s1 = scores(k_refs[2 * blk + 1][0, 0])
        g = lane_sum(s0 + s1)
        s0 = s0 + c_far
        if blk == bps - 1:
            s1 = s1 + jnp.where(last_step, bias_ref[...], c_far)
        else:
            s1 = s1 + c_far
        m = jnp.maximum(lane_max(s0), lane_max(s1))
        p0 = jnp.exp2(s0 - m)
        p1 = jnp.exp2(s1 - m)
        row = j * bps + blk
        m_s[pl.ds(row, 1)] = jnp.broadcast_to(m, stat)
        l_s[pl.ds(row, 1)] = jnp.broadcast_to(lane_sum(p0 + p1), stat)
        g_s[pl.ds(row, 1)] = jnp.broadcast_to(g, stat)
        acc_s[pl.ds(row, 1)] = (p0 * v_refs[2 * blk][0, 0] + p1 * v_refs[2 * blk + 1][0, 0])[None]

    @pl.when(last_step)
    def _():
        s_new = scores(kn_ref[0]) + rpe_ref[:, :, 0:1] * LOG2E
        lane = lax.broadcasted_iota(jnp.int32, (ATT_HEADS, ATT_HEAD_DIM, LANES), 2)
        m_s[n_blocks:n_blocks + 1] = s_new[None]
        l_s[n_blocks:n_blocks + 1] = jnp.ones(stat, F32)
        g_s[n_blocks:n_blocks + 1] = jnp.zeros(stat, F32)
        acc_s[n_blocks:n_blocks + 1] = jnp.where(lane == 0, vn_ref[0], 0.0)[None]
        rows = n_blocks + 1
        n = lax.broadcasted_iota(jnp.int32, m_s.shape, 0)
        gate = jnp.where(n < n_blocks, g_s[...], -jnp.inf)
        sel = _top_blocks(gate, n, rows, lambda jj: jj < n_blocks) | (n == n_blocks)
        mm = jnp.where(sel, m_s[...], -jnp.inf)
        w = jnp.exp2(mm - jnp.max(mm, axis=0, keepdims=True))
        den = jnp.sum(w * l_s[...], axis=0)
        num = lane_sum(jnp.sum(w * acc_s[...], axis=0))
        o_ref[0] = jnp.broadcast_to(num / den[:, :, 0:1], o_ref.shape[1:]).astype(BF16)


def _dec_att_call(page_table, q4, kn4, vn4, dec_bias, rpe_heads, cache_kT, cache_vT, layer, pages_per_step):
    n_seq, n_pages = page_table.shape
    n_blocks = n_pages * PAGE_SIZE // MOBA_BLOCK
    steps = n_pages // pages_per_step
    rows = n_blocks + 1
    tile = (ATT_HEADS, ATT_HEAD_DIM, LANES)
    vec = pl.BlockSpec((1, *tile), lambda b, j, pt: (b, 0, 0, 0))

    def page_spec(p):
        return pl.BlockSpec((1, 1, ATT_HEADS, ATT_HEAD_DIM, PAGE_SIZE),
                            lambda b, j, pt: (layer, pt[b, j * pages_per_step + p], 0, 0, 0))

    grid_spec = pltpu.PrefetchScalarGridSpec(
        num_scalar_prefetch=1,
        grid=(n_seq, steps),
        in_specs=[vec, vec, vec,
                  pl.BlockSpec((ATT_HEADS, 1, PAGE_SIZE), lambda b, j, pt: (0, 0, 0)),
                  pl.BlockSpec((ATT_HEADS, 1, RPE_BUCKETS), lambda b, j, pt: (0, 0, 0))]
                 + [page_spec(p) for p in range(pages_per_step)]
                 + [page_spec(p) for p in range(pages_per_step)],
        out_specs=vec,
        scratch_shapes=[pltpu.VMEM((rows, ATT_HEADS, 1, LANES), F32),
                        pltpu.VMEM((rows, ATT_HEADS, 1, LANES), F32),
                        pltpu.VMEM((rows, ATT_HEADS, 1, LANES), F32),
                        pltpu.VMEM((rows, *tile), F32)],
    )
    return pl.pallas_call(
        functools.partial(_dec_att_kernel, pages_per_step=pages_per_step, n_blocks=n_blocks),
        grid_spec=grid_spec,
        out_shape=jax.ShapeDtypeStruct((n_seq, *tile), BF16),
        compiler_params=_params(("arbitrary", "arbitrary")),
        name="moba_attention_sample",
    )(page_table, q4, kn4, vn4, dec_bias, rpe_heads,
      *([cache_kT] * pages_per_step), *([cache_vT] * pages_per_step))


def _dec_hg_kernel(qh_ref, fh_ref, ih_ref, gh_ref, lb_ref, gon_ref, s_ref, o_ref, so_ref):
    r = lax.broadcasted_iota(jnp.int32, (HG_DIM, HG_DIM), 0)
    c = lax.broadcasted_iota(jnp.int32, (HG_DIM, HG_DIM), 1)
    eye = r == c

    def column(x):
        return jnp.sum(jnp.where(eye, jnp.broadcast_to(x, (HG_DIM, HG_DIM)), 0.0), axis=1, keepdims=True)

    for h in range(HG_HEADS):
        cols = slice(h * HG_DIM, (h + 1) * HG_DIM)
        lb = lb_ref[:, cols]
        fpre = fh_ref[0][:, cols]
        qh = qh_ref[0][:, cols]
        gh = gh_ref[0][:, cols]
        v = ih_ref[0][:, cols]
        f = jnp.exp(_log_forget(fpre, lb))
        kk = (1.0 - lb) / (1.0 + jnp.exp(fpre))
        q = qh / (1.0 + jnp.exp(-qh))
        s_new = column(f) * s_ref[0, h] + column(kk) * v
        so_ref[0, h] = s_new
        o = jnp.sum(column(q) * s_new, axis=0, keepdims=True)
        o = _rms(o, gon_ref[:, cols]) * (gh / (1.0 + jnp.exp(-gh)))
        o_ref[0, :, cols] = o.astype(BF16)


def _dec_hg_call(qh3, fh3, ih3, gh3, lb, gon, state, layer):
    n_seq = qh3.shape[0]
    vec = pl.BlockSpec((1, 1, HG_WIDTH), lambda b: (b, 0, 0))
    par = pl.BlockSpec((None, 1, HG_WIDTH), lambda b: (layer, 0, 0))
    st_in = pl.BlockSpec((None, 1, HG_HEADS, HG_DIM, HG_DIM), lambda b: (layer, b, 0, 0, 0))
    st = pl.BlockSpec((1, HG_HEADS, HG_DIM, HG_DIM), lambda b: (b, 0, 0, 0))
    return pl.pallas_call(
        _dec_hg_kernel,
        grid=(n_seq,),
        in_specs=[vec, vec, vec, vec, par, par, st_in],
        out_specs=[vec, st],
        out_shape=[jax.ShapeDtypeStruct((n_seq, 1, HG_WIDTH), BF16),
                   jax.ShapeDtypeStruct(state.shape[1:], F32)],
        compiler_params=_params(("arbitrary",)),
        name="hgrn2_sample",
    )(qh3, fh3, ih3, gh3, lb, gon, state)


def _prepare_weights(w_in, w_out, w_up, w_down):
    a = ATT_WIDTH
    wi = w_in.astype(BF16)
    wq, wk, wv = wi[:, :, :a], wi[:, :, a:2 * a], wi[:, :, 2 * a:3 * a]
    wkp = jnp.pad(wk.reshape(DEPTH, D_MODEL, ATT_HEADS, ATT_HEAD_DIM),
                  ((0, 0), (0, 0), (0, 0), (0, HEAD_PAD - ATT_HEAD_DIM))).reshape(DEPTH, D_MODEL, -1)
    tr = lambda m: jnp.swapaxes(m, 1, 2)
    return {"wqT": tr(wq), "wkT": tr(wk), "wvT": tr(wv), "wkp": wkp, "whg": wi[:, :, 3 * a:], "w_in": wi,
            "wout": w_out.astype(BF16), "wup": w_up.astype(BF16), "wdn": w_down.astype(BF16)}


def kernel(x_prompt, x_sample, c_prompt, c_sample, cache_k, cache_v, state_hgrn, page_table, w_ada, b_ada,
           g_pre_mix, g_post_mix, g_pre_ffn, g_post_ffn, w_in, lb_param, g_onorm, w_out, w_up, w_down,
           rpe_table):
    batch, seq_len, _ = x_prompt.shape
    n_seq, dec_seq, _ = x_sample.shape
    n_pages = page_table.shape[1]
    assert dec_seq == 1 and seq_len % MOBA_BLOCK == 0 and (n_pages * PAGE_SIZE) % MOBA_BLOCK == 0
    assert seq_len // MOBA_BLOCK <= GATE_SLOTS
    tm = 512 if seq_len % 512 == 0 else MOBA_BLOCK
    pages_per_step = next(p for p in (32, 16, 8, 4, 2) if n_pages % p == 0)
    a = ATT_WIDTH
    hd = (ATT_HEADS, ATT_HEAD_DIM)

    w = _prepare_weights(w_in, w_out, w_up, w_down)
    per_layer = lambda arr: arr.reshape(DEPTH, 1, -1)
    g_pre, g_post, g_ffn, g_ffn_out = map(per_layer, (g_pre_mix, g_post_mix, g_pre_ffn, g_post_ffn))
    gains = (g_post, g_ffn, g_ffn_out)
    gon = per_layer(g_onorm)

    n_c = batch + n_seq
    c_rows = ((n_c + 7) // 8) * 8
    c_all = jnp.pad(jnp.concatenate([c_prompt, c_sample], axis=0), ((0, c_rows - n_c), (0, 0)))
    mod = _ada_call(c_all, w_ada, b_ada)
    mod_p = mod[:, :, :batch].reshape(DEPTH, 6, batch, 1, D_MODEL)
    mod_s = mod[:, :, batch:n_c].reshape(DEPTH, 6, 1, n_seq, D_MODEL)
    lb = per_layer(_lb_call(lb_param))
    rpe_flat = rpe_table.T.reshape(-1)
    rpe_heads = rpe_table.T.reshape(ATT_HEADS, 1, RPE_BUCKETS)
    bias_tiles = _bias_tile_call(rpe_flat).reshape(ATT_HEADS, 2 * MOBA_BLOCK, MOBA_BLOCK)
    dec_bias = _dec_bias_call(rpe_heads)
    cache_kT = jnp.transpose(cache_k, (0, 1, 3, 4, 2))
    cache_vT = jnp.transpose(cache_v, (0, 1, 3, 4, 2))

    xp = x_prompt.reshape(batch * seq_len, D_MODEL)
    xs = x_sample.reshape(n_seq, D_MODEL)
    k_p, v_p, s_p, k_s, v_s, s_s = [], [], [], [], [], []
    for l in range(DEPTH):
        qT, kT, vT, vTb, kaug, kmean, zhg = _in_proj_call(xp, g_pre, mod_p, w, l, batch, seq_len, tm)
        oT = _att_call(rpe_flat, qT, kmean.reshape(batch, seq_len // MOBA_BLOCK, -1), kaug, vTb, bias_tiles,
                       batch, seq_len)
        o_att = jnp.swapaxes(oT, 1, 2).reshape(batch * seq_len, a)
        o_hg, sT = _hg_call(zhg, lb, gon, l, batch, seq_len, MOBA_BLOCK)
        xp = _post_call(xp, o_att, o_hg, mod_p, gains, w, l, seq_len, tm)
        k_p.append(kT)
        v_p.append(vT)
        s_p.append(jnp.swapaxes(sT, -1, -2))

        z = _dec_in_call(xs, g_pre, mod_s, w["w_in"], l)
        part = lambda i: z[:, i * a:(i + 1) * a]
        wide = lambda i: jnp.broadcast_to(part(i).reshape(n_seq, *hd, 1), (n_seq, *hd, LANES))
        o4 = _dec_att_call(page_table, wide(0), wide(1), wide(2), dec_bias, rpe_heads, cache_kT, cache_vT,
                           l, pages_per_step)
        o_att_s = o4[..., 0].reshape(n_seq, a)
        row3 = lambda i: part(i).reshape(n_seq, 1, a)
        o_hgs, s_new = _dec_hg_call(row3(3), row3(4), row3(5), row3(6), lb, gon, state_hgrn, l)
        xs = _post_call(xs, o_att_s, o_hgs.reshape(n_seq, a), mod_s, gains, w, l, n_seq, n_seq)
        k_s.append(part(1))
        v_s.append(part(2))
        s_s.append(s_new)

    rows_out = lambda parts: jnp.transpose(jnp.stack(parts).reshape(DEPTH, batch, *hd, seq_len),
                                           (0, 1, 4, 2, 3))
    return (xp.reshape(batch, seq_len, D_MODEL),
            xs.reshape(n_seq, 1, D_MODEL),
            rows_out(k_p),
            rows_out(v_p),
            jnp.stack(s_p),
            jnp.stack(k_s).reshape(DEPTH, n_seq, 1, *hd),
            jnp.stack(v_s).reshape(DEPTH, n_seq, 1, *hd),
            jnp.stack(s_s))
```

```python
import functools
import math

import numpy as np
import jax
import jax.numpy as jnp
from jax import lax
from jax.experimental import pallas as pl
from jax.experimental.pallas import tpu as pltpu

F32 = jnp.float32
BF16 = jnp.bfloat16

D_MODEL = 1024
DEPTH = 4
ATT_HEADS = 8
ATT_HEAD_DIM = 64
ATT_WIDTH = ATT_HEADS * ATT_HEAD_DIM
MOBA_BLOCK = 256
MOBA_TOPK = 3
PAGE_SIZE = 128
RPE_BUCKETS = 32
RPE_MAX_DIST = 128
HG_HEADS = 4
HG_DIM = 128
HG_WIDTH = HG_HEADS * HG_DIM
D_FF = 4 * D_MODEL
EPS = 1e-6

LANES = 128
HEAD_PAD = 128
GATE_SLOTS = 32
ATT_HEADS_PER_STEP = 4
ATT_SPAN = 4
ATT_DEN_ROWS = 16
HG_HEADS_PER_STEP = 4
LOG2E = 1.4426950408889634
Q_SCALE = ATT_HEAD_DIM ** -0.5 * LOG2E
NEG = -30000.0
VMEM_LIMIT = 56 * 1024 * 1024

NT = (((1,), (1,)), ((), ()))


def _bucket_starts():
    max_exact = RPE_BUCKETS // 2
    d = np.arange(0, RPE_MAX_DIST + 1)
    dd = np.maximum(d, max_exact).astype(np.float32)
    large = max_exact + (np.log(dd / np.float32(max_exact)) / np.float32(math.log(RPE_MAX_DIST / max_exact))
                         * np.float32(RPE_BUCKETS - max_exact)).astype(np.int32)
    large = np.minimum(large, RPE_BUCKETS - 1)
    b = np.where(d < max_exact, d, large)
    return tuple(int(np.argmax(b >= k)) for k in range(RPE_BUCKETS))


BUCKET_STARTS = _bucket_starts()


def _params(sem):
    return pltpu.CompilerParams(dimension_semantics=sem, vmem_limit_bytes=VMEM_LIMIT)


def _rms(x, g):
    return x * lax.rsqrt(jnp.mean(x * x, axis=-1, keepdims=True) + EPS) * g


def _dot(a, b):
    return jnp.dot(a, b, preferred_element_type=F32)


def _dot_exact_lhs(a, x):
    hi = x.astype(BF16)
    r = x - hi.astype(F32)
    mid = r.astype(BF16)
    lo = (r - mid.astype(F32)).astype(BF16)
    return _dot(a, hi) + _dot(a, mid) + _dot(a, lo)


def _top_blocks(gate, n, sentinel, hit_ok):
    sel = jnp.zeros(gate.shape, jnp.bool_)
    for j in range(MOBA_TOPK):
        mx = jnp.max(gate, axis=0, keepdims=True)
        idx = jnp.min(jnp.where(gate == mx, n, sentinel), axis=0, keepdims=True)
        hit = n == idx
        sel = sel | (hit & hit_ok(j))
        gate = jnp.where(hit, -jnp.inf, gate)
    return sel


def _layer_spec(arr, layer):
    return pl.BlockSpec((None, *arr.shape[1:]), lambda *_: (layer, 0, 0))


def _mod_spec(mods, layer, j, rows_per_mod, tm):
    return pl.BlockSpec((None, None, None, *mods.shape[3:]),
                        lambda i: (layer, j, (i * tm) // rows_per_mod, 0, 0))


def _ada_kernel(c_ref, w_ref, b_ref, o_ref):
    c = c_ref[...]
    act = (c / (1.0 + jnp.exp(-c))).astype(BF16)
    o_ref[0, 0] = _dot(act, w_ref[0].astype(BF16)) + b_ref[0, 0]


def _ada_call(c_all, w_ada, b_ada):
    rows = c_all.shape[0]
    return pl.pallas_call(
        _ada_kernel,
        grid=(DEPTH, 6),
        in_specs=[pl.BlockSpec((rows, D_MODEL), lambda l, j: (0, 0)),
                  pl.BlockSpec((1, D_MODEL, D_MODEL), lambda l, j: (l, 0, j)),
                  pl.BlockSpec((1, 1, 1, D_MODEL), lambda l, j: (l, j, 0, 0))],
        out_specs=pl.BlockSpec((1, 1, rows, D_MODEL), lambda l, j: (l, j, 0, 0)),
        out_shape=jax.ShapeDtypeStruct((DEPTH, 6, rows, D_MODEL), F32),
        compiler_params=_params(("arbitrary", "arbitrary")),
        name="ada_mod",
    )(c_all, w_ada, b_ada.reshape(DEPTH, 6, 1, D_MODEL))


def _lb_kernel(p_ref, o_ref):
    p = p_ref[...]
    e = jnp.exp(p - jnp.max(p, axis=0, keepdims=True))
    sm = e / jnp.sum(e, axis=0, keepdims=True)
    acc = jnp.zeros((1, HG_WIDTH), F32)
    for l in range(DEPTH):
        o_ref[l:l + 1, :] = acc
        if l + 1 < DEPTH:
            acc = acc + sm[l + 1:l + 2, :]


def _lb_call(lb_param):
    return pl.pallas_call(
        _lb_kernel,
        out_shape=jax.ShapeDtypeStruct((DEPTH, HG_WIDTH), F32),
        name="hgrn_lower_bounds",
    )(lb_param)


def _bias_lookup(d, table):
    val = table(RPE_BUCKETS - 1)
    for b in range(RPE_BUCKETS - 2, -1, -1):
        val = jnp.where(d < BUCKET_STARTS[b + 1], table(b), val)
    return val * LOG2E


def _bias_tile_kernel(rpe_ref, o_ref):
    h = pl.program_id(0)
    kk = lax.broadcasted_iota(jnp.int32, (MOBA_BLOCK, MOBA_BLOCK), 0)
    qq = lax.broadcasted_iota(jnp.int32, (MOBA_BLOCK, MOBA_BLOCK), 1)
    d = qq - kk
    table = lambda b: rpe_ref[h * RPE_BUCKETS + b]
    o_ref[0, 0] = _bias_lookup(d + MOBA_BLOCK, table)
    o_ref[0, 1] = jnp.where(d >= 0, _bias_lookup(d, table), NEG)


def _bias_tile_call(rpe_flat):
    return pl.pallas_call(
        _bias_tile_kernel,
        grid=(ATT_HEADS,),
        in_specs=[pl.BlockSpec(memory_space=pltpu.SMEM)],
        out_specs=pl.BlockSpec((1, 2, MOBA_BLOCK, MOBA_BLOCK), lambda h: (h, 0, 0, 0)),
        out_shape=jax.ShapeDtypeStruct((ATT_HEADS, 2, MOBA_BLOCK, MOBA_BLOCK), F32),
        compiler_params=_params(("arbitrary",)),
        name="rpe_bias_tiles",
    )(rpe_flat)


def _dec_bias_kernel(rpe_ref, o_ref):
    assert BUCKET_STARTS[-1] <= PAGE_SIZE + 1
    d = PAGE_SIZE - lax.broadcasted_iota(jnp.int32, (ATT_HEADS, 1, PAGE_SIZE), 2)
    o_ref[...] = _bias_lookup(d, lambda b: rpe_ref[:, :, b:b + 1])


def _dec_bias_call(rpe_heads):
    return pl.pallas_call(
        _dec_bias_kernel,
        out_shape=jax.ShapeDtypeStruct((ATT_HEADS, 1, PAGE_SIZE), F32),
        name="rpe_bias_decode",
    )(rpe_heads)


def _in_proj_kernel(x_ref, g_ref, sc_ref, sh_ref, wqT_ref, wkT_ref, wvT_ref, wkp_ref, whg_ref,
                    qT_ref, kT_ref, vT_ref, vTb_ref, kaug_ref, kmean_ref, zhg_ref, *, blocks_per_seq):
    x = x_ref[...]
    tm = x.shape[0]
    h = (_rms(x, g_ref[...]) * (1.0 + sc_ref[...]) + sh_ref[...]).astype(BF16)
    transposed = lambda w_ref: lax.dot_general(w_ref[...], h, NT, preferred_element_type=F32)
    qT_ref[0] = (transposed(wqT_ref) * Q_SCALE).astype(BF16)
    kT_ref[0] = transposed(wkT_ref)
    vT = transposed(wvT_ref)
    vT_ref[0] = vT
    vTb_ref[0] = vT.astype(BF16)
    zhg_ref[...] = _dot(h, whg_ref[...])
    kp = _dot(h, wkp_ref[...])
    nb = tm // MOBA_BLOCK
    for r in range(nb):
        kmean_ref[r] = jnp.mean(kp[r * MOBA_BLOCK:(r + 1) * MOBA_BLOCK], axis=0, keepdims=True)
    row = lax.broadcasted_iota(jnp.int32, kp.shape, 0)
    lane = lax.broadcasted_iota(jnp.int32, kp.shape, 1) % HEAD_PAD
    blk = (pl.program_id(0) * nb + row // MOBA_BLOCK) % blocks_per_seq
    onehot = (lane == blk + ATT_HEAD_DIM) | (lane == blk + ATT_HEAD_DIM + GATE_SLOTS)
    kaug_ref[...] = jnp.where(onehot, 1.0, kp).astype(BF16)


def _in_proj_call(x2, g, mods, w, layer, batch, seq_len, tm):
    t = x2.shape[0]
    tps = seq_len // tm
    nbt = tm // MOBA_BLOCK
    hp = ATT_HEADS * HEAD_PAD
    row = lambda i: (i, 0)
    tr = pl.BlockSpec((1, ATT_WIDTH, tm), lambda i: (i // tps, 0, i % tps))
    wspec = lambda a: _layer_spec(a, layer)
    tshape = lambda dt: jax.ShapeDtypeStruct((batch, ATT_WIDTH, seq_len), dt)
    return pl.pallas_call(
        functools.partial(_in_proj_kernel, blocks_per_seq=seq_len // MOBA_BLOCK),
        grid=(t // tm,),
        in_specs=[pl.BlockSpec((tm, D_MODEL), row),
                  _layer_spec(g, layer),
                  _mod_spec(mods, layer, 1, seq_len, tm),
                  _mod_spec(mods, layer, 0, seq_len, tm),
                  wspec(w["wqT"]), wspec(w["wkT"]), wspec(w["wvT"]), wspec(w["wkp"]), wspec(w["whg"])],
        out_specs=[tr, tr, tr, tr,
                   pl.BlockSpec((tm, hp), row),
                   pl.BlockSpec((nbt, 1, hp), lambda i: (i, 0, 0)),
                   pl.BlockSpec((tm, 4 * HG_WIDTH), row)],
        out_shape=[tshape(BF16), tshape(F32), tshape(F32), tshape(BF16),
                   jax.ShapeDtypeStruct((t, hp), BF16),
                   jax.ShapeDtypeStruct((t // MOBA_BLOCK, 1, hp), F32),
                   jax.ShapeDtypeStruct((t, 4 * HG_WIDTH), F32)],
        compiler_params=_params(("arbitrary",)),
        name="in_proj",
    )(x2, g, mods, mods, w["wqT"], w["wkT"], w["wvT"], w["wkp"], w["whg"])


def _gated_queries(q, kmean, c_far, own, q_scr):
    nbs = kmean.shape[0]
    tq = q.shape[1]
    n = lax.broadcasted_iota(jnp.int32, (nbs, tq), 0)
    far = n <= own - 2
    gate = jnp.where(n < own, _dot(kmean.astype(BF16), q), -jnp.inf)
    sel = _top_blocks(gate, n, nbs, lambda j: j < own)
    c = jnp.full((nbs, tq), c_far, F32)
    c_hi = c.astype(BF16).astype(F32)
    c_lo = c - c_hi
    p_hi = jnp.where(sel, jnp.where(far, c_hi, 0.0), NEG)
    p_hi = jnp.where(n >= own, 0.0, p_hi)
    p_lo = jnp.where(sel & far, c_lo, 0.0)
    q_scr[0:ATT_HEAD_DIM, :] = q
    q_scr[ATT_HEAD_DIM:HEAD_PAD, :] = jnp.zeros((HEAD_PAD - ATT_HEAD_DIM, tq), BF16)
    q_scr[ATT_HEAD_DIM:ATT_HEAD_DIM + nbs, :] = p_hi.astype(BF16)
    q_scr[ATT_HEAD_DIM + GATE_SLOTS:ATT_HEAD_DIM + GATE_SLOTS + nbs, :] = p_lo.astype(BF16)


def _att_kernel(rpe_ref, qT_ref, kmean_ref, k_ref, vT_ref, bias_ref, o_ref, q_scr, s_a, s_b, x_a, x_b, p_scr,
                m_scr, a_scr, acc_scr, *, heads, n_key_blocks):
    qi = pl.program_id(2)
    for h in range(heads):
        head = pl.program_id(1) * heads + h
        _gated_queries(qT_ref[0, h * ATT_HEAD_DIM:(h + 1) * ATT_HEAD_DIM, :],
                       kmean_ref[0][:, h * HEAD_PAD:h * HEAD_PAD + ATT_HEAD_DIM],
                       rpe_ref[head * RPE_BUCKETS + RPE_BUCKETS - 1] * LOG2E, qi, q_scr.at[h])
    ones = jnp.ones((ATT_DEN_ROWS, ATT_SPAN * MOBA_BLOCK), BF16)

    def scores(first_block, n_blocks, near, s_scr, x_scr):
        keys = n_blocks * MOBA_BLOCK
        start = pl.multiple_of(first_block * MOBA_BLOCK, MOBA_BLOCK)
        for h in range(heads):
            s = _dot(k_ref[pl.ds(start, keys), h * HEAD_PAD:(h + 1) * HEAD_PAD], q_scr[h])
            if near:
                s = s + bias_ref[h, (2 - n_blocks) * MOBA_BLOCK:2 * MOBA_BLOCK, :]
            s_scr[h, 0:keys, :] = s
            x_scr[h] = jnp.max(s, axis=0, keepdims=True)

    def absorb(first_block, n_blocks, s_scr, x_scr):
        keys = n_blocks * MOBA_BLOCK
        start = pl.multiple_of(first_block * MOBA_BLOCK, MOBA_BLOCK)
        for h in range(heads):
            m = m_scr[h]
            m_new = jnp.maximum(m, x_scr[h])
            a_scr[h] = jnp.exp2(m - m_new)
            m_scr[h] = m_new
            p_scr[h, 0:keys, :] = jnp.exp2(s_scr[h, 0:keys, :] - m_new).astype(BF16)
        for h in range(heads):
            v = vT_ref[0, h * ATT_HEAD_DIM:(h + 1) * ATT_HEAD_DIM, pl.ds(start, keys)]
            v1 = jnp.concatenate([v, ones[:, 0:keys]], axis=0)
            acc_scr[h] = a_scr[h] * acc_scr[h] + _dot(v1, p_scr[h, 0:keys, :])

    def span(first_block, n_blocks, near):
        scores(first_block, n_blocks, near, s_a, x_a)
        absorb(first_block, n_blocks, s_a, x_a)

    m_scr[...] = jnp.full(m_scr.shape, -jnp.inf, F32)
    acc_scr[...] = jnp.zeros(acc_scr.shape, F32)

    n_far = jnp.maximum(qi - 1, 0)
    n_full = n_far // ATT_SPAN
    ahead = lambda i: jnp.minimum(i * ATT_SPAN, n_key_blocks - ATT_SPAN)

    @pl.when(qi == 0)
    def _():
        span(0, 1, True)

    @pl.when((qi >= 1) & (n_full == 0))
    def _():
        span(qi - 1, 2, True)

    @pl.when(n_full > 0)
    def _():
        scores(qi - 1, 2, True, s_b, x_b)
        scores(0, ATT_SPAN, False, s_a, x_a)
        absorb(qi - 1, 2, s_b, x_b)

    @pl.loop(0, n_full // 2)
    def _(j):
        scores(ahead(2 * j + 1), ATT_SPAN, False, s_b, x_b)
        absorb(2 * j * ATT_SPAN, ATT_SPAN, s_a, x_a)
        scores(ahead(2 * j + 2), ATT_SPAN, False, s_a, x_a)
        absorb((2 * j + 1) * ATT_SPAN, ATT_SPAN, s_b, x_b)

    @pl.when(n_full % 2 == 1)
    def _():
        absorb((n_full - 1) * ATT_SPAN, ATT_SPAN, s_a, x_a)

    done = n_full * ATT_SPAN
    size = ATT_SPAN // 2
    while size >= 1:
        @pl.when((n_far - done) & size != 0)
        def _(done=done, size=size):
            span(done, size, False)
        done = done + ((n_far - done) & size)
        size //= 2

    for h in range(heads):
        acc = acc_scr[h]
        den = acc[ATT_HEAD_DIM:ATT_HEAD_DIM + 1, :]
        o_ref[0, h * ATT_HEAD_DIM:(h + 1) * ATT_HEAD_DIM, :] = (acc[0:ATT_HEAD_DIM, :] / den).astype(BF16)


def _att_call(rpe_flat, qT, kmean, kaug, vT, bias_tiles, batch, seq_len):
    nq = seq_len // MOBA_BLOCK
    hb = ATT_HEADS_PER_STEP
    return pl.pallas_call(
        functools.partial(_att_kernel, heads=hb, n_key_blocks=nq),
        grid=(batch, ATT_HEADS // hb, nq),
        in_specs=[pl.BlockSpec(memory_space=pltpu.SMEM),
                  pl.BlockSpec((1, hb * ATT_HEAD_DIM, MOBA_BLOCK), lambda b, h, i: (b, h, i)),
                  pl.BlockSpec((1, nq, hb * HEAD_PAD), lambda b, h, i: (b, 0, h)),
                  pl.BlockSpec((seq_len, hb * HEAD_PAD), lambda b, h, i: (b, h)),
                  pl.BlockSpec((1, hb * ATT_HEAD_DIM, seq_len), lambda b, h, i: (b, h, 0)),
                  pl.BlockSpec((hb, 2 * MOBA_BLOCK, MOBA_BLOCK), lambda b, h, i: (h, 0, 0))],
        out_specs=pl.BlockSpec((1, hb * ATT_HEAD_DIM, MOBA_BLOCK), lambda b, h, i: (b, h, i)),
        out_shape=jax.ShapeDtypeStruct((batch, ATT_WIDTH, seq_len), BF16),
        scratch_shapes=[pltpu.VMEM((hb, HEAD_PAD, MOBA_BLOCK), BF16),
                        pltpu.VMEM((hb, ATT_SPAN * MOBA_BLOCK, MOBA_BLOCK), F32),
                        pltpu.VMEM((hb, ATT_SPAN * MOBA_BLOCK, MOBA_BLOCK), F32),
                        pltpu.VMEM((hb, 1, MOBA_BLOCK), F32),
                        pltpu.VMEM((hb, 1, MOBA_BLOCK), F32),
                        pltpu.VMEM((hb, ATT_SPAN * MOBA_BLOCK, MOBA_BLOCK), BF16),
                        pltpu.VMEM((hb, 1, MOBA_BLOCK), F32),
                        pltpu.VMEM((hb, 1, MOBA_BLOCK), F32),
                        pltpu.VMEM((hb, ATT_HEAD_DIM + ATT_DEN_ROWS, MOBA_BLOCK), F32)],
        compiler_params=_params(("arbitrary", "arbitrary", "arbitrary")),
        name="moba_attention",
    )(rpe_flat, qT, kmean, kaug, vT, bias_tiles)


def _log_forget(fpre, lb):
    log_sig = jnp.minimum(fpre, 0.0) - jnp.log(1.0 + jnp.exp(-jnp.abs(fpre)))
    a1 = jnp.log(lb)
    a2 = jnp.log1p(-lb) + log_sig
    return jnp.maximum(a1, a2) + jnp.log(1.0 + jnp.exp(-jnp.abs(a1 - a2)))


def _mid_rows(b_scr, b, row, m, ts):
    if m == 1:
        return jnp.where((row & 1) != 0, pltpu.roll(b, 1, 0), b)
    bcast = lambda r, n: jnp.broadcast_to(b_scr[r:r + 1, :], (n, HG_DIM))
    if m == 2:
        lo = jnp.concatenate([bcast(8 * j + 1, 8) for j in range(ts // 8)], axis=0)
        hi = jnp.concatenate([bcast(8 * j + 5, 8) for j in range(ts // 8)], axis=0)
        return jnp.where((row & 4) == 0, lo, hi)
    return jnp.concatenate([bcast(2 * m * j + m - 1, 2 * m) for j in range(ts // (2 * m))], axis=0)


def _halves(q, kk, row, m, ts):
    if m < 8:
        return jnp.where((row & m) != 0, q, kk)
    pieces = [(q if (j & 1) else kk)[j * m:(j + 1) * m] for j in range(ts // m)]
    return jnp.concatenate(pieces, axis=0)


def _hg_kernel(qh_ref, fh_ref, ih_ref, gh_ref, lb_ref, gon_ref, o_ref, sT_ref, b_scr, a_scr, lvl_scr, *, heads):
    ts = qh_ref.shape[0]
    n_levels = ts.bit_length() - 1

    @pl.when((pl.program_id(0) == 0) & (pl.program_id(1) == 0) & (pl.program_id(2) == 0))
    def _():
        t = lax.broadcasted_iota(jnp.int32, (ts, ts), 0)
        s = lax.broadcasted_iota(jnp.int32, (ts, ts), 1)
        x = t ^ s
        lvl = jnp.zeros((ts, ts), jnp.int32)
        for i in range(1, n_levels):
            lvl = lvl + (x >= (1 << i)).astype(jnp.int32)
        lvl_scr[...] = jnp.where(s < t, lvl, -1)

    @pl.when(pl.program_id(2) == 0)
    def _():
        sT_ref[...] = jnp.zeros_like(sT_ref)

    r = lax.broadcasted_iota(jnp.int32, (ts, ts), 0)
    c = lax.broadcasted_iota(jnp.int32, (ts, ts), 1)
    tri = jnp.where(c <= r, 1.0, 0.0).astype(BF16)
    row = lax.broadcasted_iota(jnp.int32, (ts, HG_DIM), 0)

    for hh in range(heads):
        cols = slice(hh * HG_DIM, (hh + 1) * HG_DIM)
        b_ref, a_ref = b_scr.at[hh], a_scr.at[hh]
        lb = lb_ref[:, cols]
        fpre = fh_ref[:, cols]
        qh = qh_ref[:, cols]
        v = ih_ref[:, cols]
        logf = _log_forget(fpre, lb)
        kk = (1.0 - lb) / (1.0 + jnp.exp(fpre))
        q = qh / (1.0 + jnp.exp(-qh))
        b = _dot_exact_lhs(tri, logf) * LOG2E
        b_ref[...] = b
        b_last = b_ref[ts - 1:ts, :]

        a_ref[...] = jnp.zeros((ts, ts), F32)
        for i in range(n_levels):
            m = 1 << i
            e = jnp.exp2(-jnp.abs(b - _mid_rows(b_ref, b, row, m, ts)))
            x = (_halves(q, kk, row, m, ts) * e).astype(BF16)
            p = lax.dot_general(x, x, NT, preferred_element_type=F32)
            if m < 8:
                owned = [(slice(t0, t0 + LANES), slice(t0, t0 + LANES)) for t0 in range(0, ts, LANES)]
            else:
                owned = [(slice(2 * m * j + m, 2 * m * (j + 1)),
                          slice(2 * m * j // LANES * LANES, (2 * m * j // LANES + 1) * LANES))
                         for j in range(ts // (2 * m))]
            for rws, cls in owned:
                a_ref[rws, cls] = jnp.where(lvl_scr[rws, cls] == i, p[rws, cls], a_ref[rws, cls])
        o = _dot(a_ref[...].astype(BF16), v.astype(BF16)) + jnp.sum(q * kk, axis=-1, keepdims=True) * v

        sT = sT_ref[0, hh]
        o = o + lax.dot_general((q * jnp.exp2(b)).astype(BF16), sT.astype(BF16), NT,
                                preferred_element_type=F32)
        k_dec = (kk * jnp.exp2(b_last - b)).astype(BF16)
        sT_ref[0, hh] = sT * jnp.exp2(b_last) + _dot(v.T.astype(BF16), k_dec)

        gh = gh_ref[:, cols]
        o = _rms(o, gon_ref[:, cols]) * (gh / (1.0 + jnp.exp(-gh)))
        o_ref[:, cols] = o.astype(BF16)


def _hg_call(zhg, lb, gon, layer, batch, seq_len, ts):
    t = zhg.shape[0]
    nt = seq_len // ts
    hb = HG_HEADS_PER_STEP
    groups = HG_HEADS // hb
    part = lambda p: pl.BlockSpec((ts, hb * HG_DIM), lambda b, g, i: (b * nt + i, p * groups + g))
    vec = pl.BlockSpec((None, 1, hb * HG_DIM), lambda b, g, i: (layer, 0, g))
    return pl.pallas_call(
        functools.partial(_hg_kernel, heads=hb),
        grid=(batch, groups, nt),
        in_specs=[part(0), part(1), part(2), part(3), vec, vec],
        out_specs=[pl.BlockSpec((ts, hb * HG_DIM), lambda b, g, i: (b * nt + i, g)),
                   pl.BlockSpec((1, hb, HG_DIM, HG_DIM), lambda b, g, i: (b, g, 0, 0))],
        out_shape=[jax.ShapeDtypeStruct((t, HG_WIDTH), BF16),
                   jax.ShapeDtypeStruct((batch, HG_HEADS, HG_DIM, HG_DIM), F32)],
        scratch_shapes=[pltpu.VMEM((hb, ts, HG_DIM), F32), pltpu.VMEM((hb, ts, ts), F32),
                        pltpu.VMEM((ts, ts), jnp.int32)],
        compiler_params=_params(("arbitrary", "arbitrary", "arbitrary")),
        name="hgrn2_prompt",
    )(zhg, zhg, zhg, zhg, lb, gon)


def _post_kernel(x_ref, oa_ref, oh_ref, ga1_ref, sc2_ref, sh2_ref, ga2_ref, gpm_ref, gpf_ref, gqf_ref,
                 woa_ref, woh_ref, wup_ref, wdn_ref, out_ref, *, ff_chunk):
    x = x_ref[...]
    y = _dot(oa_ref[...], woa_ref[...]) + _dot(oh_ref[...], woh_ref[...])
    x1 = x + ga1_ref[...] * _rms(y, gpm_ref[...])
    h2 = (_rms(x1, gpf_ref[...]) * (1.0 + sc2_ref[...]) + sh2_ref[...]).astype(BF16)
    m = jnp.zeros(x.shape, F32)
    for c in range(D_FF // ff_chunk):
        cols = slice(c * ff_chunk, (c + 1) * ff_chunk)
        u = jnp.square(jnp.maximum(_dot(h2, wup_ref[:, cols]), 0.0)).astype(BF16)
        m = m + _dot(u, wdn_ref[cols, :])
    out_ref[...] = x1 + ga2_ref[...] * _rms(m, gqf_ref[...])


def _post_call(x2, oa, oh, mods, gains, w, layer, rows_per_mod, tm):
    t = x2.shape[0]
    row = lambda i: (i, 0)
    mod = lambda j: _mod_spec(mods, layer, j, rows_per_mod, tm)
    half = lambda part: pl.BlockSpec((None, ATT_WIDTH, D_MODEL), lambda i: (layer, part, 0))
    return pl.pallas_call(
        functools.partial(_post_kernel, ff_chunk=1024),
        grid=(t // tm,),
        in_specs=[pl.BlockSpec((tm, D_MODEL), row),
                  pl.BlockSpec((tm, ATT_WIDTH), row),
                  pl.BlockSpec((tm, HG_WIDTH), row),
                  mod(2), mod(4), mod(3), mod(5)] + [_layer_spec(g, layer) for g in gains]
                 + [half(0), half(1), _layer_spec(w["wup"], layer), _layer_spec(w["wdn"], layer)],
        out_specs=pl.BlockSpec((tm, D_MODEL), row),
        out_shape=jax.ShapeDtypeStruct((t, D_MODEL), F32),
        compiler_params=_params(("arbitrary",)),
        name="out_proj_mlp",
    )(x2, oa, oh, mods, mods, mods, mods, *gains, w["wout"], w["wout"], w["wup"], w["wdn"])


def _dec_in_kernel(x_ref, g_ref, sc_ref, sh_ref, w_ref, z_ref):
    h = (_rms(x_ref[...], g_ref[...]) * (1.0 + sc_ref[...]) + sh_ref[...]).astype(BF16)
    z_ref[...] = _dot(h, w_ref[...])


def _dec_in_call(x2, g, mods, w_in, layer):
    rows = x2.shape[0]
    n_in = w_in.shape[2]
    tn = 512
    full = pl.BlockSpec((rows, D_MODEL), lambda j: (0, 0))
    mod = lambda c: pl.BlockSpec((None, None, None, rows, D_MODEL), lambda j: (layer, c, 0, 0, 0))
    return pl.pallas_call(
        _dec_in_kernel,
        grid=(n_in // tn,),
        in_specs=[full, _layer_spec(g, layer), mod(1), mod(0),
                  pl.BlockSpec((None, D_MODEL, tn), lambda j: (layer, 0, j))],
        out_specs=pl.BlockSpec((rows, tn), lambda j: (0, j)),
        out_shape=jax.ShapeDtypeStruct((rows, n_in), F32),
        compiler_params=_params(("arbitrary",)),
        name="in_proj_sample",
    )(x2, g, mods, mods, w_in)


def _dec_att_kernel(pt_ref, q_ref, kn_ref, vn_ref, bias_ref, rpe_ref, *refs, pages_per_step, n_blocks):
    k_refs = refs[:pages_per_step]
    v_refs = refs[pages_per_step:2 * pages_per_step]
    o_ref = refs[2 * pages_per_step]
    m_s, l_s, g_s, acc_s = refs[2 * pages_per_step + 1:]
    j = pl.program_id(1)
    last_step = j == pl.num_programs(1) - 1
    bps = pages_per_step // 2
    q = q_ref[0] * Q_SCALE
    c_far = rpe_ref[:, :, RPE_BUCKETS - 1:RPE_BUCKETS] * LOG2E
    stat = (1, ATT_HEADS, 1, LANES)

    scores = lambda k_page: jnp.sum(k_page * q, axis=1, keepdims=True)
    lane_sum = lambda x: jnp.sum(x, axis=-1, keepdims=True)
    lane_max = lambda x: jnp.max(x, axis=-1, keepdims=True)

    for blk in range(bps):
        s0 = scores(k_refs[2 * blk][0, 0])
        s1 = scores(k_refs[2 * blk + 1][0, 0])
        g = lane_sum(s0 + s1)
        s0 = s0 + c_far
        if blk == bps - 1:
            s1 = s1 + jnp.where(last_step, bias_ref[...], c_far)
        else:
            s1 = s1 + c_far
        m = jnp.maximum(lane_max(s0), lane_max(s1))
        p0 = jnp.exp2(s0 - m)
        p1 = jnp.exp2(s1 - m)
        row = j * bps + blk
        m_s[pl.ds(row, 1)] = jnp.broadcast_to(m, stat)
        l_s[pl.ds(row, 1)] = jnp.broadcast_to(lane_sum(p0 + p1), stat)
        g_s[pl.ds(row, 1)] = jnp.broadcast_to(g, stat)
        acc_s[pl.ds(row, 1)] = (p0 * v_refs[2 * blk][0, 0] + p1 * v_refs[2 * blk + 1][0, 0])[None]

    @pl.when(last_step)
    def _():
        s_new = scores(kn_ref[0]) + rpe_ref[:, :, 0:1] * LOG2E
        lane = lax.broadcasted_iota(jnp.int32, (ATT_HEADS, ATT_HEAD_DIM, LANES), 2)
        m_s[n_blocks:n_blocks + 1] = s_new[None]
        l_s[n_blocks:n_blocks + 1] = jnp.ones(stat, F32)
        g_s[n_blocks:n_blocks + 1] = jnp.zeros(stat, F32)
        acc_s[n_blocks:n_blocks + 1] = jnp.where(lane == 0, vn_ref[0], 0.0)[None]
        rows = n_blocks + 1
        n = lax.broadcasted_iota(jnp.int32, m_s.shape, 0)
        gate = jnp.where(n < n_blocks, g_s[...], -jnp.inf)
        sel = _top_blocks(gate, n, rows, lambda jj: jj < n_blocks) | (n == n_blocks)
        mm = jnp.where(sel, m_s[...], -jnp.inf)
        w = jnp.exp2(mm - jnp.max(mm, axis=0, keepdims=True))
        den = jnp.sum(w * l_s[...], axis=0)
        num = lane_sum(jnp.sum(w * acc_s[...], axis=0))
        o_ref[0] = jnp.broadcast_to(num / den[:, :, 0:1], o_ref.shape[1:]).astype(BF16)


def _dec_att_call(page_table, q4, kn4, vn4, dec_bias, rpe_heads, cache_kT, cache_vT, layer, pages_per_step):
    n_seq, n_pages = page_table.shape
    n_blocks = n_pages * PAGE_SIZE // MOBA_BLOCK
    steps = n_pages // pages_per_step
    rows = n_blocks + 1
    tile = (ATT_HEADS, ATT_HEAD_DIM, LANES)
    vec = pl.BlockSpec((1, *tile), lambda b, j, pt: (b, 0, 0, 0))

    def page_spec(p):
        return pl.BlockSpec((1, 1, ATT_HEADS, ATT_HEAD_DIM, PAGE_SIZE),
                            lambda b, j, pt: (layer, pt[b, j * pages_per_step + p], 0, 0, 0))

    grid_spec = pltpu.PrefetchScalarGridSpec(
        num_scalar_prefetch=1,
        grid=(n_seq, steps),
        in_specs=[vec, vec, vec,
                  pl.BlockSpec((ATT_HEADS, 1, PAGE_SIZE), lambda b, j, pt: (0, 0, 0)),
                  pl.BlockSpec((ATT_HEADS, 1, RPE_BUCKETS), lambda b, j, pt: (0, 0, 0))]
                 + [page_spec(p) for p in range(pages_per_step)]
                 + [page_spec(p) for p in range(pages_per_step)],
        out_specs=vec,
        scratch_shapes=[pltpu.VMEM((rows, ATT_HEADS, 1, LANES), F32),
                        pltpu.VMEM((rows, ATT_HEADS, 1, LANES), F32),
                        pltpu.VMEM((rows, ATT_HEADS, 1, LANES), F32),
                        pltpu.VMEM((rows, *tile), F32)],
    )
    return pl.pallas_call(
        functools.partial(_dec_att_kernel, pages_per_step=pages_per_step, n_blocks=n_blocks),
        grid_spec=grid_spec,
        out_shape=jax.ShapeDtypeStruct((n_seq, *tile), BF16),
        compiler_params=_params(("arbitrary", "arbitrary")),
        name="moba_attention_sample",
    )(page_table, q4, kn4, vn4, dec_bias, rpe_heads,
      *([cache_kT] * pages_per_step), *([cache_vT] * pages_per_step))


def _dec_hg_kernel(qh_ref, fh_ref, ih_ref, gh_ref, lb_ref, gon_ref, s_ref, o_ref, so_ref):
    r = lax.broadcasted_iota(jnp.int32, (HG_DIM, HG_DIM), 0)
    c = lax.broadcasted_iota(jnp.int32, (HG_DIM, HG_DIM), 1)
    eye = r == c

    def column(x):
        return jnp.sum(jnp.where(eye, jnp.broadcast_to(x, (HG_DIM, HG_DIM)), 0.0), axis=1, keepdims=True)

    for h in range(HG_HEADS):
        cols = slice(h * HG_DIM, (h + 1) * HG_DIM)
        lb = lb_ref[:, cols]
        fpre = fh_ref[0][:, cols]
        qh = qh_ref[0][:, cols]
        gh = gh_ref[0][:, cols]
        v = ih_ref[0][:, cols]
        f = jnp.exp(_log_forget(fpre, lb))
        kk = (1.0 - lb) / (1.0 + jnp.exp(fpre))
        q = qh / (1.0 + jnp.exp(-qh))
        s_new = column(f) * s_ref[0, h] + column(kk) * v
        so_ref[0, h] = s_new
        o = jnp.sum(column(q) * s_new, axis=0, keepdims=True)
        o = _rms(o, gon_ref[:, cols]) * (gh / (1.0 + jnp.exp(-gh)))
        o_ref[0, :, cols] = o.astype(BF16)


def _dec_hg_call(qh3, fh3, ih3, gh3, lb, gon, state, layer):
    n_seq = qh3.shape[0]
    vec = pl.BlockSpec((1, 1, HG_WIDTH), lambda b: (b, 0, 0))
    par = pl.BlockSpec((None, 1, HG_WIDTH), lambda b: (layer, 0, 0))
    st_in = pl.BlockSpec((None, 1, HG_HEADS, HG_DIM, HG_DIM), lambda b: (layer, b, 0, 0, 0))
    st = pl.BlockSpec((1, HG_HEADS, HG_DIM, HG_DIM), lambda b: (b, 0, 0, 0))
    return pl.pallas_call(
        _dec_hg_kernel,
        grid=(n_seq,),
        in_specs=[vec, vec, vec, vec, par, par, st_in],
        out_specs=[vec, st],
        out_shape=[jax.ShapeDtypeStruct((n_seq, 1, HG_WIDTH), BF16),
                   jax.ShapeDtypeStruct(state.shape[1:], F32)],
        compiler_params=_params(("arbitrary",)),
        name="hgrn2_sample",
    )(qh3, fh3, ih3, gh3, lb, gon, state)


def _prepare_weights(w_in, w_out, w_up, w_down):
    a = ATT_WIDTH
    wi = w_in.astype(BF16)
    wq, wk, wv = wi[:, :, :a], wi[:, :, a:2 * a], wi[:, :, 2 * a:3 * a]
    wkp = jnp.pad(wk.reshape(DEPTH, D_MODEL, ATT_HEADS, ATT_HEAD_DIM),
                  ((0, 0), (0, 0), (0, 0), (0, HEAD_PAD - ATT_HEAD_DIM))).reshape(DEPTH, D_MODEL, -1)
    tr = lambda m: jnp.swapaxes(m, 1, 2)
    return {"wqT": tr(wq), "wkT": tr(wk), "wvT": tr(wv), "wkp": wkp, "whg": wi[:, :, 3 * a:], "w_in": wi,
            "wout": w_out.astype(BF16), "wup": w_up.astype(BF16), "wdn": w_down.astype(BF16)}


def kernel(x_prompt, x_sample, c_prompt, c_sample, cache_k, cache_v, state_hgrn, page_table, w_ada, b_ada,
           g_pre_mix, g_post_mix, g_pre_ffn, g_post_ffn, w_in, lb_param, g_onorm, w_out, w_up, w_down,
           rpe_table):
    batch, seq_len, _ = x_prompt.shape
    n_seq, dec_seq, _ = x_sample.shape
    n_pages = page_table.shape[1]
    assert dec_seq == 1 and seq_len % MOBA_BLOCK == 0 and (n_pages * PAGE_SIZE) % MOBA_BLOCK == 0
    assert seq_len // MOBA_BLOCK <= GATE_SLOTS
    tm = 512 if seq_len % 512 == 0 else MOBA_BLOCK
    pages_per_step = next(p for p in (32, 16, 8, 4, 2) if n_pages % p == 0)
    a = ATT_WIDTH
    hd = (ATT_HEADS, ATT_HEAD_DIM)

    w = _prepare_weights(w_in, w_out, w_up, w_down)
    per_layer = lambda arr: arr.reshape(DEPTH, 1, -1)
    g_pre, g_post, g_ffn, g_ffn_out = map(per_layer, (g_pre_mix, g_post_mix, g_pre_ffn, g_post_ffn))
    gains = (g_post, g_ffn, g_ffn_out)
    gon = per_layer(g_onorm)

    n_c = batch + n_seq
    c_rows = ((n_c + 7) // 8) * 8
    c_all = jnp.pad(jnp.concatenate([c_prompt, c_sample], axis=0), ((0, c_rows - n_c), (0, 0)))
    mod = _ada_call(c_all, w_ada, b_ada)
    mod_p = mod[:, :, :batch].reshape(DEPTH, 6, batch, 1, D_MODEL)
    mod_s = mod[:, :, batch:n_c].reshape(DEPTH, 6, 1, n_seq, D_MODEL)
    lb = per_layer(_lb_call(lb_param))
    rpe_flat = rpe_table.T.reshape(-1)
    rpe_heads = rpe_table.T.reshape(ATT_HEADS, 1, RPE_BUCKETS)
    bias_tiles = _bias_tile_call(rpe_flat).reshape(ATT_HEADS, 2 * MOBA_BLOCK, MOBA_BLOCK)
    dec_bias = _dec_bias_call(rpe_heads)
    cache_kT = jnp.transpose(cache_k, (0, 1, 3, 4, 2))
    cache_vT = jnp.transpose(cache_v, (0, 1, 3, 4, 2))

    xp = x_prompt.reshape(batch * seq_len, D_MODEL)
    xs = x_sample.reshape(n_seq, D_MODEL)
    k_p, v_p, s_p, k_s, v_s, s_s = [], [], [], [], [], []
    for l in range(DEPTH):
        qT, kT, vT, vTb, kaug, kmean, zhg = _in_proj_call(xp, g_pre, mod_p, w, l, batch, seq_len, tm)
        oT = _att_call(rpe_flat, qT, kmean.reshape(batch, seq_len // MOBA_BLOCK, -1), kaug, vTb, bias_tiles,
                       batch, seq_len)
        o_att = jnp.swapaxes(oT, 1, 2).reshape(batch * seq_len, a)
        o_hg, sT = _hg_call(zhg, lb, gon, l, batch, seq_len, MOBA_BLOCK)
        xp = _post_call(xp, o_att, o_hg, mod_p, gains, w, l, seq_len, tm)
        k_p.append(kT)
        v_p.append(vT)
        s_p.append(jnp.swapaxes(sT, -1, -2))

        z = _dec_in_call(xs, g_pre, mod_s, w["w_in"], l)
        part = lambda i: z[:, i * a:(i + 1) * a]
        wide = lambda i: jnp.broadcast_to(part(i).reshape(n_seq, *hd, 1), (n_seq, *hd, LANES))
        o4 = _dec_att_call(page_table, wide(0), wide(1), wide(2), dec_bias, rpe_heads, cache_kT, cache_vT,
                           l, pages_per_step)
        o_att_s = o4[..., 0].reshape(n_seq, a)
        row3 = lambda i: part(i).reshape(n_seq, 1, a)
        o_hgs, s_new = _dec_hg_call(row3(3), row3(4), row3(5), row3(6), lb, gon, state_hgrn, l)
        xs = _post_call(xs, o_att_s, o_hgs.reshape(n_seq, a), mod_s, gains, w, l, n_seq, n_seq)
        k_s.append(part(1))
        v_s.append(part(2))
        s_s.append(s_new)

    rows_out = lambda parts: jnp.transpose(jnp.stack(parts).reshape(DEPTH, batch, *hd, seq_len),
                                           (0, 1, 4, 2, 3))
    return (xp.reshape(batch, seq_len, D_MODEL),
            xs.reshape(n_seq, 1, D_MODEL),
            rows_out(k_p),
            rows_out(v_p),
            jnp.stack(s_p),
            jnp.stack(k_s).reshape(DEPTH, n_seq, 1, *hd),
            jnp.stack(v_s).reshape(DEPTH, n_seq, 1, *hd),
            jnp.stack(s_s))
```
